```python
import jax, jax.numpy as jnp
from jax import lax
import numpy as np

D_MODEL = 1024
BATCH = 16
SEQ = 2048
DEPTH = 2
DEC_BATCH = 4
DEC_SEQ = 8192
PAST_LEN = 128

N_GROUPS = 4
D_POOL = D_MODEL
D_CONV = D_MODEL
D_FNET = D_MODEL
G_POOL = D_POOL // N_GROUPS
G_FNET = D_FNET // N_GROUPS
POOL_WINDOWS = (2, 4, 8, 16)
CONV_WIDTH = 31
CONV_PAD = CONV_WIDTH // 2
D_FF = 4 * D_MODEL
D_PLE = 256
N_BRANCH = 3
D_IN = D_POOL + 2 * D_CONV + D_FNET + N_BRANCH * D_MODEL
SPLITS = (D_POOL, D_POOL + 2 * D_CONV, D_POOL + 2 * D_CONV + D_FNET)
EPS = 1e-6

kernel_name = "hybrid_pool_conv_fourier_encoder"


def rmsnorm(x, g):
    xf = x.astype(jnp.float32)
    y = xf * lax.rsqrt(jnp.mean(xf * xf, axis=-1, keepdims=True) + EPS)
    return (y * g.astype(jnp.float32)).astype(x.dtype)


def layernorm(x, g, b):
    xf = x.astype(jnp.float32)
    mu = jnp.mean(xf, axis=-1, keepdims=True)
    xc = xf - mu
    var = jnp.mean(xc * xc, axis=-1, keepdims=True)
    y = xc * lax.rsqrt(var + EPS) * g.astype(jnp.float32) + b.astype(jnp.float32)
    return y.astype(x.dtype)


def pool_mixer(u, w, scale):
    B, S, _ = u.shape
    uf = u.astype(jnp.float32).reshape(B, S, N_GROUPS, G_POOL)
    c = jnp.concatenate([jnp.zeros((B, 1, N_GROUPS, G_POOL), jnp.float32),
                         jnp.cumsum(uf, axis=1)], axis=1)
    t = np.arange(S)
    means = []
    for gi, win in enumerate(POOL_WINDOWS):
        lo = np.maximum(t - win // 2, 0)
        hi = np.minimum(t + win // 2 - 1, S - 1)
        cnt = (hi - lo + 1).astype(np.float32)
        cg = c[:, :, gi, :]
        means.append((cg[:, hi + 1] - cg[:, lo]) / jnp.asarray(cnt)[None, :, None])
    pooled = (jnp.stack(means, axis=2) - uf).astype(u.dtype)
    y = jnp.einsum('bsgc,gcd->bsgd', pooled, w).reshape(B, S, D_MODEL)
    return y * scale


def conv_mixer(u2, conv_w, conv_b, ln_g, ln_b, w_out):
    a, g = jnp.split(u2, 2, axis=-1)
    v = a * jax.nn.sigmoid(g)
    v = lax.conv_general_dilated(v, conv_w[:, None, :], window_strides=(1,),
                                 padding=[(CONV_PAD, CONV_PAD)],
                                 dimension_numbers=('NWC', 'WIO', 'NWC'),
                                 feature_group_count=D_CONV) + conv_b
    v = jax.nn.silu(layernorm(v, ln_g, ln_b))
    return v @ w_out


def fnet_mixer(u, w):
    B, S, _ = u.shape
    uf = u.astype(jnp.float32).reshape(B, S, N_GROUPS, G_FNET)
    f = jnp.fft.fft2(uf, axes=(1, 3), norm='ortho').real.astype(u.dtype)
    return jnp.einsum('bsgc,gcd->bsgd', f, w).reshape(B, S, D_MODEL)


def trunk(x, p, norm_mix_g, w_in, b_gate, pool_w, pool_scale, conv_w, conv_b,
          conv_ln_g, conv_ln_b, conv_out_w, fnet_w, w_o, norm_ff_g, w_ff1, w_ff2,
          norm_ple_g, w_ple_gate, w_ple_proj, final_norm_g):
    h = x
    B, S, _ = x.shape
    for i in range(DEPTH):
        xn = rmsnorm(h, norm_mix_g[i])
        z = xn @ w_in[i]
        u_pool, u_conv, u_fnet, gate_pre = jnp.split(z, SPLITS, axis=-1)
        gates = jax.nn.sigmoid(gate_pre + b_gate[i]).reshape(B, S, N_BRANCH, D_MODEL)
        y_a = pool_mixer(u_pool, pool_w[i], pool_scale[i])
        y_b = conv_mixer(u_conv, conv_w[i], conv_b[i], conv_ln_g[i], conv_ln_b[i], conv_out_w[i])
        y_c = fnet_mixer(u_fnet, fnet_w[i])
        merged = gates[:, :, 0] * y_a + gates[:, :, 1] * y_b + gates[:, :, 2] * y_c
        h = h + merged @ w_o[i]
        hf = rmsnorm(h, norm_ff_g[i]) @ w_ff1[i]
        h = h + jnp.square(jax.nn.relu(hf)) @ w_ff2[i]
        pg = jax.nn.sigmoid(rmsnorm(h, norm_ple_g[i]) @ w_ple_gate[i])
        h = h + (p[i] @ w_ple_proj[i]) * pg
    return rmsnorm(h, final_norm_g)


def setup_inputs(seed: int = 0) -> dict:
    key = jax.random.key(seed)
    ks = jax.random.split(key, 24)

    def nrm(k, shape, scale):
        return jax.random.normal(k, shape, jnp.float32) * scale

    def gain(k, shape):
        return 1.0 + 0.02 * jax.random.normal(k, shape, jnp.float32)

    L, D = DEPTH, D_MODEL
    return {
        "x_prompt": nrm(ks[0], (BATCH, SEQ, D), 1.0),
        "x_sample": nrm(ks[1], (DEC_BATCH, DEC_SEQ, D), 1.0),
        "p_prompt": nrm(ks[2], (DEPTH, BATCH, SEQ, D_PLE), 1.0),
        "p_sample": nrm(ks[3], (DEPTH, DEC_BATCH, DEC_SEQ, D_PLE), 1.0),
        "norm_mix_g": gain(ks[4], (L, D)),
        "w_in": nrm(ks[5], (L, D, D_IN), D ** -0.5),
        "b_gate": nrm(ks[6], (L, N_BRANCH * D), 0.02),
        "pool_w": nrm(ks[7], (L, N_GROUPS, G_POOL, G_POOL), G_POOL ** -0.5),
        "pool_scale": gain(ks[8], (L, D)),
        "conv_w": nrm(ks[9], (L, CONV_WIDTH, D_CONV), CONV_WIDTH ** -0.5),
        "conv_b": nrm(ks[10], (L, D_CONV), 0.02),
        "conv_ln_g": gain(ks[11], (L, D_CONV)),
        "conv_ln_b": nrm(ks[12], (L, D_CONV), 0.02),
        "conv_out_w": nrm(ks[13], (L, D_CONV, D), D_CONV ** -0.5),
        "fnet_w": nrm(ks[14], (L, N_GROUPS, G_FNET, G_FNET), G_FNET ** -0.5),
        "w_o": nrm(ks[15], (L, D, D), D ** -0.5),
        "norm_ff_g": gain(ks[16], (L, D)),
        "w_ff1": nrm(ks[17], (L, D, D_FF), D ** -0.5),
        "w_ff2": nrm(ks[18], (L, D_FF, D), D_FF ** -0.5),
        "norm_ple_g": gain(ks[19], (L, D)),
        "w_ple_gate": nrm(ks[20], (L, D, D), D ** -0.5),
        "w_ple_proj": nrm(ks[21], (L, D_PLE, D), D_PLE ** -0.5),
        "final_norm_g": gain(ks[22], (D,)),
    }


def reference(x_prompt, x_sample, p_prompt, p_sample, norm_mix_g, w_in, b_gate, pool_w,
              pool_scale, conv_w, conv_b, conv_ln_g, conv_ln_b, conv_out_w, fnet_w, w_o,
              norm_ff_g, w_ff1, w_ff2, norm_ple_g, w_ple_gate, w_ple_proj, final_norm_g):
    y_prompt = trunk(x_prompt, p_prompt, norm_mix_g, w_in, b_gate, pool_w, pool_scale, conv_w,
                     conv_b, conv_ln_g, conv_ln_b, conv_out_w, fnet_w, w_o, norm_ff_g, w_ff1,
                     w_ff2, norm_ple_g, w_ple_gate, w_ple_proj, final_norm_g)
    y_sample = trunk(x_sample, p_sample, norm_mix_g, w_in, b_gate, pool_w, pool_scale, conv_w,
                     conv_b, conv_ln_g, conv_ln_b, conv_out_w, fnet_w, w_o, norm_ff_g, w_ff1,
                     w_ff2, norm_ple_g, w_ple_gate, w_ple_proj, final_norm_g)
    return (y_prompt, y_sample)
```

```python
import functools
import math

import numpy as np
import jax
import jax.numpy as jnp
from jax import lax
from jax.experimental import pallas as pl
from jax.experimental.pallas import tpu as pltpu

N_GROUPS = 4
POOL_WINDOWS = (2, 4, 8, 16)
CONV_WIDTH = 31
CONV_PAD = CONV_WIDTH // 2
N_BRANCH = 3
EPS = 1e-6

HALO = 16
DFT_N2 = 128
DFT_ROWS = 128
VMEM_LIMIT_BYTES = 56 * 1024 * 1024

F32 = jnp.float32
BF16 = jnp.bfloat16


def _const_spec(shape):
    nd = len(shape)
    return pl.BlockSpec(shape, lambda *_: (0,) * nd, pipeline_mode=pl.Buffered(1))


def _params(n_grid):
    return pltpu.CompilerParams(dimension_semantics=("parallel",) * n_grid,
                                vmem_limit_bytes=VMEM_LIMIT_BYTES)


def _rms(x, g):
    ms = jnp.mean(x * x, axis=-1, keepdims=True)
    return x * lax.rsqrt(ms + EPS) * g


def _dot(a, b):
    return jnp.dot(a, b, preferred_element_type=F32)


def _fold_kernel(c_ref, s_ref, w_ref, o_ref):
    w = w_ref[0, 0]
    g = w.shape[-1]
    o_ref[0, 0, :, :g] = jnp.dot(c_ref[...], w, preferred_element_type=F32,
                                 precision=lax.Precision.HIGHEST)
    o_ref[0, 0, :, g:] = jnp.dot(s_ref[...], w, preferred_element_type=F32,
                                 precision=lax.Precision.HIGHEST)


def _fold_fnet(fnet_w):
    L, G, C, _ = fnet_w.shape
    idx = np.arange(C)
    ang = 2.0 * np.pi * ((idx[:, None] * idx[None, :]) % C) / C
    cmat = jnp.asarray((np.cos(ang) / math.sqrt(C)).astype(np.float32))
    smat = jnp.asarray((np.sin(ang) / math.sqrt(C)).astype(np.float32))
    return pl.pallas_call(
        _fold_kernel,
        grid=(L, G),
        in_specs=[pl.BlockSpec((C, C), lambda l, g: (0, 0)),
                  pl.BlockSpec((C, C), lambda l, g: (0, 0)),
                  pl.BlockSpec((1, 1, C, C), lambda l, g: (l, g, 0, 0))],
        out_specs=pl.BlockSpec((1, 1, C, 2 * C), lambda l, g: (l, g, 0, 0)),
        out_shape=jax.ShapeDtypeStruct((L, G, C, 2 * C), F32),
        compiler_params=_params(2),
        name="fold_fnet",
    )(cmat, smat, fnet_w)


def _inproj_kernel(h_ref, g_ref, win_ref, poolw_ref, wcs_ref, a_ref, v_ref, pq_ref):
    d = h_ref.shape[-1]
    gc = d // N_GROUPS
    xn = _rms(h_ref[0], g_ref[...]).astype(BF16)
    u_pool = _dot(xn, win_ref[:, 0:d]).astype(BF16)
    for gi in range(N_GROUPS):
        cs = slice(gi * gc, (gi + 1) * gc)
        a_ref[0, :, cs] = _dot(u_pool[:, cs], poolw_ref[gi]).astype(BF16)
    ca = _dot(xn, win_ref[:, d:2 * d])
    cg = _dot(xn, win_ref[:, 2 * d:3 * d])
    v_ref[0] = (ca * jax.nn.sigmoid(cg)).astype(BF16)
    u_f = _dot(xn, win_ref[:, 3 * d:4 * d]).astype(BF16)
    for gi in range(N_GROUPS):
        cs = slice(gi * gc, (gi + 1) * gc)
        pq = _dot(u_f[:, cs], wcs_ref[gi])
        pq_ref[0, 0, :, cs] = pq[:, :gc].astype(BF16)
        pq_ref[0, 1, :, cs] = pq[:, gc:].astype(BF16)


def _inproj(h, g, win, poolw, wcs, tile):
    B, S, D = h.shape
    gc = D // N_GROUPS
    tok = lambda b, i: (b, i, 0)
    return pl.pallas_call(
        _inproj_kernel,
        grid=(B, S // tile),
        in_specs=[pl.BlockSpec((1, tile, D), tok),
                  _const_spec((1, D)),
                  _const_spec((D, 4 * D)),
                  _const_spec((N_GROUPS, gc, gc)),
                  _const_spec((N_GROUPS, gc, 2 * gc))],
        out_specs=[pl.BlockSpec((1, tile, D), tok),
                   pl.BlockSpec((1, tile, D), tok),
                   pl.BlockSpec((1, 2, tile, D), lambda b, i: (b, 0, i, 0))],
        out_shape=[jax.ShapeDtypeStruct((B, S, D), BF16),
                   jax.ShapeDtypeStruct((B, S, D), BF16),
                   jax.ShapeDtypeStruct((B, 2, S, D), BF16)],
        compiler_params=_params(2),
        name="inproj",
    )(h, g, win, poolw, wcs)


def _dft1_kernel(g1_ref, x_ref, o_ref):
    o_ref[0] = _dot(g1_ref[...], x_ref[0]).astype(BF16)


def _dft2_kernel(g2_ref, t_ref, o_ref):
    d = t_ref.shape[-1]
    for kk in range(t_ref.shape[1]):
        o_ref[0, :, kk * d:(kk + 1) * d] = _dot(g2_ref[kk], t_ref[0, kk]).astype(BF16)


def _dft_tables(S):
    n2 = DFT_N2
    n1 = S // n2
    rep = DFT_ROWS // (2 * n1)
    i1 = np.arange(n1)
    ang1 = 2.0 * np.pi * ((i1[:, None] * i1[None, :]) % n1) / n1
    c1, s1 = np.cos(ang1), np.sin(ang1)
    g1 = np.block([[c1, -s1], [s1, c1]])
    g1 = np.kron(np.eye(rep), g1).astype(np.float32)
    k = (i1[:, None] + n1 * np.arange(n2)[None, :])
    ang2 = 2.0 * np.pi * ((k[:, :, None] * np.arange(n2)[None, None, :]) % S) / S
    g2 = np.concatenate([np.cos(ang2), -np.sin(ang2)], axis=-1) / math.sqrt(S)
    return n1, rep, jnp.asarray(g1), jnp.asarray(g2.astype(np.float32))


def _seq_dft(pq, tables):
    B, _, S, D = pq.shape
    n1, rep, g1, g2 = tables
    n2 = DFT_N2
    cols = n2 * D // rep
    x = pq.reshape(B, 2, n1, rep, cols).transpose(0, 3, 1, 2, 4).reshape(B, DFT_ROWS, cols)
    cb = 8192
    t = pl.pallas_call(
        _dft1_kernel,
        grid=(B, cols // cb),
        in_specs=[_const_spec((DFT_ROWS, DFT_ROWS)),
                  pl.BlockSpec((1, DFT_ROWS, cb), lambda b, j: (b, 0, j))],
        out_specs=pl.BlockSpec((1, DFT_ROWS, cb), lambda b, j: (b, 0, j)),
        out_shape=jax.ShapeDtypeStruct((B, DFT_ROWS, cols), BF16),
        compiler_params=_params(2),
        name="dft_stage1",
    )(g1.astype(BF16), x)
    t = t.reshape(B, rep, 2, n1, n2 // rep, D).transpose(0, 3, 2, 1, 4, 5).reshape(B, n1, 2 * n2, D)
    kb = min(n1, 16)
    y = pl.pallas_call(
        _dft2_kernel,
        grid=(B, n1 // kb),
        in_specs=[pl.BlockSpec((kb, n2, 2 * n2), lambda b, j: (j, 0, 0)),
                  pl.BlockSpec((1, kb, 2 * n2, D), lambda b, j: (b, j, 0, 0))],
        out_specs=pl.BlockSpec((1, n2, kb * D), lambda b, j: (b, 0, j)),
        out_shape=jax.ShapeDtypeStruct((B, n2, n1 * D), BF16),
        compiler_params=_params(2),
        name="dft_stage2",
    )(g2.astype(BF16), t)
    return y.reshape(B, S, D)


def _mix_kernel(h_ref, am_ref, ap_ref, an_ref, vm_ref, vp_ref, vn_ref, yc_ref,
                gmix_ref, wg_ref, bg_ref, pscale_ref, cw_ref, cb_ref, lng_ref, lnb_ref,
                cow_ref, wo_ref, o_ref, aext, vext, cbuf, *, seq_len):
    tile, d = h_ref.shape[1], h_ref.shape[2]
    gc = d // N_GROUPS
    i = pl.program_id(1)
    first = i == 0
    last = i == pl.num_programs(1) - 1

    aext[0:HALO] = jnp.where(first, 0.0, ap_ref[0].astype(F32))
    aext[HALO:HALO + tile] = am_ref[0].astype(F32)
    aext[HALO + tile:] = jnp.where(last, 0.0, an_ref[0].astype(F32))
    vext[0:HALO] = jnp.where(first, 0.0, vp_ref[0].astype(F32))
    vext[HALO:HALO + tile] = vm_ref[0].astype(F32)
    vext[HALO + tile:] = jnp.where(last, 0.0, vn_ref[0].astype(F32))

    h = h_ref[0]
    xn = _rms(h, gmix_ref[...]).astype(BF16)

    def gate(br):
        cs = slice(br * d, (br + 1) * d)
        return jax.nn.sigmoid(_dot(xn, wg_ref[:, cs]) + bg_ref[:, cs])

    rc = 8
    for c in range(tile // rc):
        base = c * rc + HALO - CONV_PAD
        acc = vext[base:base + rc, :] * cw_ref[0:1, :]
        for j in range(1, CONV_WIDTH):
            acc = acc + vext[base + j:base + j + rc, :] * cw_ref[j:j + 1, :]
        cbuf[c * rc:(c + 1) * rc, :] = acc
    cv = cbuf[...] + cb_ref[...]
    mu = jnp.mean(cv, axis=-1, keepdims=True)
    xc = cv - mu
    var = jnp.mean(xc * xc, axis=-1, keepdims=True)
    ln = xc * lax.rsqrt(var + EPS) * lng_ref[...] + lnb_ref[...]
    y_b = _dot((ln * jax.nn.sigmoid(ln)).astype(BF16), cow_ref[...])
    merged = gate(1) * y_b

    t = i * tile + lax.broadcasted_iota(jnp.int32, (tile, gc), 0)
    ya = []
    for gi, win in enumerate(POOL_WINDOWS):
        cs = slice(gi * gc, (gi + 1) * gc)
        half = win // 2
        acc = aext[HALO - half:HALO - half + tile, cs]
        for dlt in range(-half + 1, half):
            acc = acc + aext[HALO + dlt:HALO + dlt + tile, cs]
        lo = jnp.maximum(t - half, 0)
        hi = jnp.minimum(t + half - 1, seq_len - 1)
        cnt = (hi - lo + 1).astype(F32)
        ya.append(acc / cnt - aext[HALO:HALO + tile, cs])
    y_a = jnp.concatenate(ya, axis=-1) * pscale_ref[...]
    merged = merged + gate(0) * y_a
    merged = merged + gate(2) * yc_ref[0].astype(F32)

    o_ref[0] = h + _dot(merged.astype(BF16), wo_ref[...])


def _mix(h, a, v, yc, gmix, wg, bg, pscale, cw, cb, lng, lnb, cow, wo, tile):
    B, S, D = h.shape
    nh = tile // HALO
    n_halo_blocks = S // HALO
    tok = lambda b, i: (b, i, 0)
    prev = lambda b, i: (b, jnp.maximum(i * nh - 1, 0), 0)
    nxt = lambda b, i: (b, jnp.minimum((i + 1) * nh, n_halo_blocks - 1), 0)
    main_spec = pl.BlockSpec((1, tile, D), tok)
    prev_spec = pl.BlockSpec((1, HALO, D), prev)
    next_spec = pl.BlockSpec((1, HALO, D), nxt)
    return pl.pallas_call(
        functools.partial(_mix_kernel, seq_len=S),
        grid=(B, S // tile),
        in_specs=[main_spec,
                  main_spec, prev_spec, next_spec,
                  main_spec, prev_spec, next_spec,
                  main_spec,
                  _const_spec((1, D)),
                  _const_spec((D, N_BRANCH * D)),
                  _const_spec((1, N_BRANCH * D)),
                  _const_spec((1, D)),
                  _const_spec((CONV_WIDTH + 1, D)),
                  _const_spec((1, D)),
                  _const_spec((1, D)),
                  _const_spec((1, D)),
                  _const_spec((D, D)),
                  _const_spec((D, D))],
        out_specs=main_spec,
        out_shape=jax.ShapeDtypeStruct((B, S, D), F32),
        scratch_shapes=[pltpu.VMEM((tile + 2 * HALO, D), F32),
                        pltpu.VMEM((tile + 2 * HALO, D), F32),
                        pltpu.VMEM((tile, D), F32)],
        compiler_params=_params(2),
        name="mix",
    )(h, a, a, a, v, v, v, yc, gmix, wg, bg, pscale, cw, cb, lng, lnb, cow, wo)


def _ffn_kernel(h_ref, p_ref, gff_ref, w1_ref, w2_ref, gple_ref, wpg_ref, wpp_ref, gfin_ref,
                o_ref, *, final):
    h = h_ref[0]
    xn = _rms(h, gff_ref[...]).astype(BF16)
    hf = jnp.maximum(_dot(xn, w1_ref[...]), 0.0)
    h = h + _dot((hf * hf).astype(BF16), w2_ref[...])
    pg = jax.nn.sigmoid(_dot(_rms(h, gple_ref[...]).astype(BF16), wpg_ref[...]))
    h = h + _dot(p_ref[0, 0].astype(BF16), wpp_ref[...]) * pg
    if final:
        h = _rms(h, gfin_ref[...])
    o_ref[0] = h


def _ffn(h, p, layer, gff, w1, w2, gple, wpg, wpp, gfin, tile, final):
    B, S, D = h.shape
    dff = w1.shape[-1]
    dple = p.shape[-1]
    tok = lambda b, i: (b, i, 0)
    return pl.pallas_call(
        functools.partial(_ffn_kernel, final=final),
        grid=(B, S // tile),
        in_specs=[pl.BlockSpec((1, tile, D), tok),
                  pl.BlockSpec((1, 1, tile, dple), lambda b, i: (layer, b, i, 0)),
                  _const_spec((1, D)),
                  _const_spec((D, dff)),
                  _const_spec((dff, D)),
                  _const_spec((1, D)),
                  _const_spec((D, D)),
                  _const_spec((dple, D)),
                  _const_spec((1, D))],
        out_specs=pl.BlockSpec((1, tile, D), tok),
        out_shape=jax.ShapeDtypeStruct((B, S, D), F32),
        compiler_params=_params(2),
        name="ffn_ple",
    )(h, p, gff, w1, w2, gple, wpg, wpp, gfin)


def _trunk(x, p, w, tables):
    depth = w["w_in_a"].shape[0]
    h = x
    for l in range(depth):
        a, v, pq = _inproj(h, w["norm_mix_g"][l], w["w_in_a"][l], w["pool_w"][l], w["wcs"][l], tile=512)
        yc = _seq_dft(pq, tables)
        h = _mix(h, a, v, yc, w["norm_mix_g"][l], w["w_in_g"][l], w["b_gate"][l], w["pool_scale"][l],
                 w["conv_w"][l], w["conv_b"][l], w["conv_ln_g"][l], w["conv_ln_b"][l],
                 w["conv_out_w"][l], w["w_o"][l], tile=256)
        h = _ffn(h, p, l, w["norm_ff_g"][l], w["w_ff1"][l], w["w_ff2"][l], w["norm_ple_g"][l],
                 w["w_ple_gate"][l], w["w_ple_proj"][l], w["final_norm_g"], tile=512,
                 final=(l == depth - 1))
    return h


def kernel(x_prompt, x_sample, p_prompt, p_sample, norm_mix_g, w_in, b_gate, pool_w, pool_scale,
           conv_w, conv_b, conv_ln_g, conv_ln_b, conv_out_w, fnet_w, w_o, norm_ff_g, w_ff1, w_ff2,
           norm_ple_g, w_ple_gate, w_ple_proj, final_norm_g):
    L, D, _ = w_in.shape
    row = lambda t: t.reshape(t.shape[0], 1, t.shape[-1])
    w = {
        "norm_mix_g": row(norm_mix_g),
        "w_in_a": w_in[:, :, :4 * D].astype(BF16),
        "w_in_g": w_in[:, :, 4 * D:].astype(BF16),
        "b_gate": row(b_gate),
        "pool_w": pool_w.astype(BF16),
        "pool_scale": row(pool_scale),
        "conv_w": jnp.pad(conv_w, ((0, 0), (0, 1), (0, 0))),
        "conv_b": row(conv_b),
        "conv_ln_g": row(conv_ln_g),
        "conv_ln_b": row(conv_ln_b),
        "conv_out_w": conv_out_w.astype(BF16),
        "wcs": _fold_fnet(fnet_w).astype(BF16),
        "w_o": w_o.astype(BF16),
        "norm_ff_g": row(norm_ff_g),
        "w_ff1": w_ff1.astype(BF16),
        "w_ff2": w_ff2.astype(BF16),
        "norm_ple_g": row(norm_ple_g),
        "w_ple_gate": w_ple_gate.astype(BF16),
        "w_ple_proj": w_ple_proj.astype(BF16),
        "final_norm_g": final_norm_g.reshape(1, D),
    }
    y_prompt = _trunk(x_prompt, p_prompt, w, _dft_tables(x_prompt.shape[1]))
    y_sample = _trunk(x_sample, p_sample, w, _dft_tables(x_sample.shape[1]))
    return (y_prompt, y_sample)
```

```python
import functools
import math

import numpy as np
import jax
import jax.numpy as jnp
from jax import lax
from jax.experimental import pallas as pl
from jax.experimental.pallas import tpu as pltpu

N_GROUPS = 4
POOL_WINDOWS = (2, 4, 8, 16)
CONV_WIDTH = 31
CONV_PAD = CONV_WIDTH // 2
N_BRANCH = 3
EPS = 1e-6

LANE = 128
SUBLANE = 8
HALO = 16
DFT_N2 = 128
DFT_ROWS = 128
VMEM_LIMIT_BYTES = 56 * 1024 * 1024

F32 = jnp.float32
BF16 = jnp.bfloat16


def _const_spec(shape):
    nd = len(shape)
    return pl.BlockSpec(shape, lambda *_: (0,) * nd, pipeline_mode=pl.Buffered(1))


def _params(n_grid):
    return pltpu.CompilerParams(dimension_semantics=("parallel",) * n_grid,
                                vmem_limit_bytes=VMEM_LIMIT_BYTES)


def _rms(x, g):
    ms = jnp.mean(x * x, axis=-1, keepdims=True)
    return x * lax.rsqrt(ms + EPS) * g


def _dot(a, b):
    return jnp.dot(a, b, preferred_element_type=F32)


def _fold_kernel(c_ref, s_ref, w_ref, o_ref):
    w = w_ref[0, 0]
    g = w.shape[-1]
    o_ref[0, 0, :, :g] = jnp.dot(c_ref[...], w, preferred_element_type=F32,
                                 precision=lax.Precision.HIGHEST)
    o_ref[0, 0, :, g:] = jnp.dot(s_ref[...], w, preferred_element_type=F32,
                                 precision=lax.Precision.HIGHEST)


def _fold_fnet(fnet_w):
    L, G, C, _ = fnet_w.shape
    idx = np.arange(C)
    ang = 2.0 * np.pi * ((idx[:, None] * idx[None, :]) % C) / C
    cmat = jnp.asarray((np.cos(ang) / math.sqrt(C)).astype(np.float32))
    smat = jnp.asarray((np.sin(ang) / math.sqrt(C)).astype(np.float32))
    return pl.pallas_call(
        _fold_kernel,
        grid=(L, G),
        in_specs=[pl.BlockSpec((C, C), lambda l, g: (0, 0)),
                  pl.BlockSpec((C, C), lambda l, g: (0, 0)),
                  pl.BlockSpec((1, 1, C, C), lambda l, g: (l, g, 0, 0))],
        out_specs=pl.BlockSpec((1, 1, C, 2 * C), lambda l, g: (l, g, 0, 0)),
        out_shape=jax.ShapeDtypeStruct((L, G, C, 2 * C), F32),
        compiler_params=_params(2),
        name="fold_fnet",
    )(cmat, smat, fnet_w)


def _inproj_kernel(h_ref, g_ref, win_ref, poolw_ref, wcs_ref, a_ref, v_ref, pq_ref):
    tile, d = h_ref.shape[1], h_ref.shape[2]
    gc = d // N_GROUPS
    xn = _rms(h_ref[0], g_ref[...]).astype(BF16)
    u_pool = _dot(xn, win_ref[:, 0:d]).astype(BF16)
    for gi in range(N_GROUPS):
        cs = slice(gi * gc, (gi + 1) * gc)
        a_ref[0, :, cs] = _dot(u_pool[:, cs], poolw_ref[gi]).astype(BF16)
    ca = _dot(xn, win_ref[:, d:2 * d])
    cg = _dot(xn, win_ref[:, 2 * d:3 * d])
    v = ca * jax.nn.sigmoid(cg)
    nj = d // LANE
    for j in range(nj):
        v_ref[0, pl.ds(j, tile, stride=nj), :] = v[:, j * LANE:(j + 1) * LANE]
    u_f = _dot(xn, win_ref[:, 3 * d:4 * d]).astype(BF16)
    for gi in range(N_GROUPS):
        cs = slice(gi * gc, (gi + 1) * gc)
        pq = _dot(u_f[:, cs], wcs_ref[gi])
        pq_ref[0, 0, :, cs] = pq[:, :gc].astype(BF16)
        pq_ref[0, 1, :, cs] = pq[:, gc:].astype(BF16)


def _inproj(h, g, win, poolw, wcs, tile):
    B, S, D = h.shape
    gc = D // N_GROUPS
    tok = lambda b, i: (b, i, 0)
    return pl.pallas_call(
        _inproj_kernel,
        grid=(B, S // tile),
        in_specs=[pl.BlockSpec((1, tile, D), tok),
                  _const_spec((1, D)),
                  _const_spec((D, 4 * D)),
                  _const_spec((N_GROUPS, gc, gc)),
                  _const_spec((N_GROUPS, gc, 2 * gc))],
        out_specs=[pl.BlockSpec((1, tile, D), tok),
                   pl.BlockSpec((1, tile * D // LANE, LANE), tok),
                   pl.BlockSpec((1, 2, tile, D), lambda b, i: (b, 0, i, 0))],
        out_shape=[jax.ShapeDtypeStruct((B, S, D), BF16),
                   jax.ShapeDtypeStruct((B, S * D // LANE, LANE), F32),
                   jax.ShapeDtypeStruct((B, 2, S, D), BF16)],
        compiler_params=_params(2),
        name="inproj",
    )(h, g, win, poolw, wcs)


def _dft1_kernel(g1_ref, x_ref, o_ref):
    o_ref[0] = _dot(g1_ref[...], x_ref[0]).astype(BF16)


def _dft2_kernel(g2_ref, t_ref, o_ref):
    d = t_ref.shape[-1]
    for kk in range(t_ref.shape[1]):
        o_ref[0, :, kk * d:(kk + 1) * d] = _dot(g2_ref[kk], t_ref[0, kk]).astype(BF16)


def _dft_tables(S):
    n2 = DFT_N2
    n1 = S // n2
    rep = DFT_ROWS // (2 * n1)
    i1 = np.arange(n1)
    ang1 = 2.0 * np.pi * ((i1[:, None] * i1[None, :]) % n1) / n1
    c1, s1 = np.cos(ang1), np.sin(ang1)
    g1 = np.block([[c1, -s1], [s1, c1]])
    g1 = np.kron(np.eye(rep), g1).astype(np.float32)
    k = (i1[:, None] + n1 * np.arange(n2)[None, :])
    ang2 = 2.0 * np.pi * ((k[:, :, None] * np.arange(n2)[None, None, :]) % S) / S
    g2 = np.concatenate([np.cos(ang2), -np.sin(ang2)], axis=-1) / math.sqrt(S)
    return n1, rep, jnp.asarray(g1), jnp.asarray(g2.astype(np.float32))


def _seq_dft(pq, tables):
    B, _, S, D = pq.shape
    n1, rep, g1, g2 = tables
    n2 = DFT_N2
    cols = n2 * D // rep
    x = pq.reshape(B, 2, n1, rep, cols).transpose(0, 3, 1, 2, 4).reshape(B, DFT_ROWS, cols)
    cb = 8192
    t = pl.pallas_call(
        _dft1_kernel,
        grid=(B, cols // cb),
        in_specs=[_const_spec((DFT_ROWS, DFT_ROWS)),
                  pl.BlockSpec((1, DFT_ROWS, cb), lambda b, j: (b, 0, j))],
        out_specs=pl.BlockSpec((1, DFT_ROWS, cb), lambda b, j: (b, 0, j)),
        out_shape=jax.ShapeDtypeStruct((B, DFT_ROWS, cols), BF16),
        compiler_params=_params(2),
        name="dft_stage1",
    )(g1.astype(BF16), x)
    t = t.reshape(B, rep, 2, n1, n2 // rep, D).transpose(0, 3, 2, 1, 4, 5).reshape(B, n1, 2 * n2, D)
    kb = min(n1, 16)
    y = pl.pallas_call(
        _dft2_kernel,
        grid=(B, n1 // kb),
        in_specs=[pl.BlockSpec((kb, n2, 2 * n2), lambda b, j: (j, 0, 0)),
                  pl.BlockSpec((1, kb, 2 * n2, D), lambda b, j: (b, j, 0, 0))],
        out_specs=pl.BlockSpec((1, n2, kb * D), lambda b, j: (b, 0, j)),
        out_shape=jax.ShapeDtypeStruct((B, n2, n1 * D), BF16),
        compiler_params=_params(2),
        name="dft_stage2",
    )(g2.astype(BF16), t)
    return y.reshape(B, S, D)


def _mix_kernel(h_ref, am_ref, ap_ref, an_ref, vm_ref, vp_ref, vn_ref, yc_ref,
                gmix_ref, wg_ref, bg_ref, pscale_ref, cw_ref, cb_ref, lng_ref, lnb_ref,
                cow_ref, wo_ref, o_ref, aext, vext, cbuf, *, seq_len):
    tile, d = h_ref.shape[1], h_ref.shape[2]
    gc = d // N_GROUPS
    i = pl.program_id(1)
    first = i == 0
    last = i == pl.num_programs(1) - 1

    aext[0:HALO] = jnp.where(first, 0.0, ap_ref[0].astype(F32))
    aext[HALO:HALO + tile] = am_ref[0].astype(F32)
    aext[HALO + tile:] = jnp.where(last, 0.0, an_ref[0].astype(F32))
    vext[0:HALO] = jnp.where(first, 0.0, vp_ref[0])
    vext[HALO:HALO + tile] = vm_ref[0]
    vext[HALO + tile:] = jnp.where(last, 0.0, vn_ref[0])

    h = h_ref[0]
    xn = _rms(h, gmix_ref[...]).astype(BF16)

    def gate(br):
        cs = slice(br * d, (br + 1) * d)
        return jax.nn.sigmoid(_dot(xn, wg_ref[:, cs]) + bg_ref[:, cs])

    rc = 8
    nj = d // LANE
    for c in range(tile // rc):
        base = c * rc + HALO - CONV_PAD
        acc = vext[base:base + rc] * cw_ref[0]
        for j in range(1, CONV_WIDTH):
            acc = acc + vext[base + j:base + j + rc] * cw_ref[j]
        cbuf[c * rc * nj:(c + 1) * rc * nj, :] = acc.reshape(rc * nj, LANE)
    cv = jnp.concatenate([cbuf[pl.ds(j, tile, stride=nj), :] for j in range(nj)], axis=-1)
    cv = cv + cb_ref[...]
    mu = jnp.mean(cv, axis=-1, keepdims=True)
    xc = cv - mu
    var = jnp.mean(xc * xc, axis=-1, keepdims=True)
    ln = xc * lax.rsqrt(var + EPS) * lng_ref[...] + lnb_ref[...]
    y_b = _dot((ln * jax.nn.sigmoid(ln)).astype(BF16), cow_ref[...])
    merged = gate(1) * y_b

    t = i * tile + lax.broadcasted_iota(jnp.int32, (tile, gc), 0)
    ya = []
    for gi, win in enumerate(POOL_WINDOWS):
        cs = slice(gi * gc, (gi + 1) * gc)
        half = win // 2
        acc = aext[HALO - half:HALO - half + tile, cs]
        for dlt in range(-half + 1, half):
            acc = acc + aext[HALO + dlt:HALO + dlt + tile, cs]
        lo = jnp.maximum(t - half, 0)
        hi = jnp.minimum(t + half - 1, seq_len - 1)
        cnt = (hi - lo + 1).astype(F32)
        ya.append(acc / cnt - aext[HALO:HALO + tile, cs])
    y_a = jnp.concatenate(ya, axis=-1) * pscale_ref[...]
    merged = merged + gate(0) * y_a
    merged = merged + gate(2) * yc_ref[0].astype(F32)

    o_ref[0] = h + _dot(merged.astype(BF16), wo_ref[...])


def _mix(h, a, v, yc, gmix, wg, bg, pscale, cw, cb, lng, lnb, cow, wo, tile):
    B, S, D = h.shape
    nh = tile // HALO
    n_halo_blocks = S // HALO
    tok = lambda b, i: (b, i, 0)
    prev = lambda b, i: (b, jnp.maximum(i * nh - 1, 0), 0)
    nxt = lambda b, i: (b, jnp.minimum((i + 1) * nh, n_halo_blocks - 1), 0)
    main_spec = pl.BlockSpec((1, tile, D), tok)
    prev_spec = pl.BlockSpec((1, HALO, D), prev)
    next_spec = pl.BlockSpec((1, HALO, D), nxt)
    nj = D // LANE
    v = v.reshape(B, S, nj, LANE)
    vmain_spec = pl.BlockSpec((1, tile, nj, LANE), lambda b, i: tok(b, i) + (0,))
    vprev_spec = pl.BlockSpec((1, HALO, nj, LANE), lambda b, i: prev(b, i) + (0,))
    vnext_spec = pl.BlockSpec((1, HALO, nj, LANE), lambda b, i: nxt(b, i) + (0,))
    cw = cw.reshape(cw.shape[0], nj, LANE)
    return pl.pallas_call(
        functools.partial(_mix_kernel, seq_len=S),
        grid=(B, S // tile),
        in_specs=[main_spec,
                  main_spec, prev_spec, next_spec,
                  vmain_spec, vprev_spec, vnext_spec,
                  main_spec,
                  _const_spec((1, D)),
                  _const_spec((D, N_BRANCH * D)),
                  _const_spec((1, N_BRANCH * D)),
                  _const_spec((1, D)),
                  _const_spec((CONV_WIDTH + 1, nj, LANE)),
                  _const_spec((1, D)),
                  _const_spec((1, D)),
                  _const_spec((1, D)),
                  _const_spec((D, D)),
                  _const_spec((D, D))],
        out_specs=main_spec,
        out_shape=jax.ShapeDtypeStruct((B, S, D), F32),
        scratch_shapes=[pltpu.VMEM((tile + 2 * HALO, D), F32),
                        pltpu.VMEM((tile + 2 * HALO, nj, LANE), F32),
                        pltpu.VMEM((tile * nj, LANE), F32)],
        compiler_params=_params(2),
        name="mix",
    )(h, a, a, a, v, v, v, yc, gmix, wg, bg, pscale, cw, cb, lng, lnb, cow, wo)


def _ffn_kernel(h_ref, p_ref, gff_ref, w1_ref, w2_ref, gple_ref, wpg_ref, wpp_ref, gfin_ref,
                o_ref, *, final):
    h = h_ref[0]
    xn = _rms(h, gff_ref[...]).astype(BF16)
    hf = jnp.maximum(_dot(xn, w1_ref[...]), 0.0)
    h = h + _dot((hf * hf).astype(BF16), w2_ref[...])
    pg = jax.nn.sigmoid(_dot(_rms(h, gple_ref[...]).astype(BF16), wpg_ref[...]))
    h = h + _dot(p_ref[0, 0].astype(BF16), wpp_ref[...]) * pg
    if final:
        h = _rms(h, gfin_ref[...])
    o_ref[0] = h


def _ffn(h, p, layer, gff, w1, w2, gple, wpg, wpp, gfin, tile, final):
    B, S, D = h.shape
    dff = w1.shape[-1]
    dple = p.shape[-1]
    tok = lambda b, i: (b, i, 0)
    return pl.pallas_call(
        functools.partial(_ffn_kernel, final=final),
        grid=(B, S // tile),
        in_specs=[pl.BlockSpec((1, tile, D), tok),
                  pl.BlockSpec((1, 1, tile, dple), lambda b, i: (layer, b, i, 0)),
                  _const_spec((1, D)),
                  _const_spec((D, dff)),
                  _const_spec((dff, D)),
                  _const_spec((1, D)),
                  _const_spec((D, D)),
                  _const_spec((dple, D)),
                  _const_spec((1, D))],
        out_specs=pl.BlockSpec((1, tile, D), tok),
        out_shape=jax.ShapeDtypeStruct((B, S, D), F32),
        compiler_params=_params(2),
        name="ffn_ple",
    )(h, p, gff, w1, w2, gple, wpg, wpp, gfin)


def _trunk(x, p, w, tables):
    depth = w["w_in_a"].shape[0]
    h = x
    for l in range(depth):
        a, v, pq = _inproj(h, w["norm_mix_g"][l], w["w_in_a"][l], w["pool_w"][l], w["wcs"][l], tile=512)
        yc = _seq_dft(pq, tables)
        h = _mix(h, a, v, yc, w["norm_mix_g"][l], w["w_in_g"][l], w["b_gate"][l], w["pool_scale"][l],
                 w["conv_w"][l], w["conv_b"][l], w["conv_ln_g"][l], w["conv_ln_b"][l],
                 w["conv_out_w"][l], w["w_o"][l], tile=256)
        h = _ffn(h, p, l, w["norm_ff_g"][l], w["w_ff1"][l], w["w_ff2"][l], w["norm_ple_g"][l],
                 w["w_ple_gate"][l], w["w_ple_proj"][l], w["final_norm_g"], tile=512,
                 final=(l == depth - 1))
    return h


def kernel(x_prompt, x_sample, p_prompt, p_sample, norm_mix_g, w_in, b_gate, pool_w, pool_scale,
           conv_w, conv_b, conv_ln_g, conv_ln_b, conv_out_w, fnet_w, w_o, norm_ff_g, w_ff1, w_ff2,
           norm_ple_g, w_ple_gate, w_ple_proj, final_norm_g):
    L, D, _ = w_in.shape
    row = lambda t: t.reshape(t.shape[0], 1, t.shape[-1])
    w = {
        "norm_mix_g": row(norm_mix_g),
        "w_in_a": w_in[:, :, :4 * D].astype(BF16),
        "w_in_g": w_in[:, :, 4 * D:].astype(BF16),
        "b_gate": row(b_gate),
        "pool_w": pool_w.astype(BF16),
        "pool_scale": row(pool_scale),
        "conv_w": jnp.pad(conv_w, ((0, 0), (0, 1), (0, 0))),
        "conv_b": row(conv_b),
        "conv_ln_g": row(conv_ln_g),
        "conv_ln_b": row(conv_ln_b),
        "conv_out_w": conv_out_w.astype(BF16),
        "wcs": _fold_fnet(fnet_w).astype(BF16),
        "w_o": w_o.astype(BF16),
        "norm_ff_g": row(norm_ff_g),
        "w_ff1": w_ff1.astype(BF16),
        "w_ff2": w_ff2.astype(BF16),
        "norm_ple_g": row(norm_ple_g),
        "w_ple_gate": w_ple_gate.astype(BF16),
        "w_ple_proj": w_ple_proj.astype(BF16),
        "final_norm_g": final_norm_g.reshape(1, D),
    }
    y_prompt = _trunk(x_prompt, p_prompt, w, _dft_tables(x_prompt.shape[1]))
    y_sample = _trunk(x_sample, p_sample, w, _dft_tables(x_sample.shape[1]))
    return (y_prompt, y_sample)
```

```python
import functools
import math

import numpy as np
import jax
import jax.numpy as jnp
from jax import lax
from jax.experimental import pallas as pl
from jax.experimental.pallas import tpu as pltpu

N_GROUPS = 4
POOL_WINDOWS = (2, 4, 8, 16)
CONV_WIDTH = 31
CONV_PAD = CONV_WIDTH // 2
N_BRANCH = 3
EPS = 1e-6

LANE = 128
SUBLANE = 8
HALO = 16
DFT_N2 = 128
DFT_ROWS = 128
DFT_BLOCK = 16
VMEM_LIMIT_BYTES = 56 * 1024 * 1024

F32 = jnp.float32
BF16 = jnp.bfloat16


def _const_spec(shape):
    nd = len(shape)
    return pl.BlockSpec(shape, lambda *_: (0,) * nd, pipeline_mode=pl.Buffered(1))


def _params(n_grid):
    return pltpu.CompilerParams(dimension_semantics=("parallel",) * n_grid,
                                vmem_limit_bytes=VMEM_LIMIT_BYTES)


def _rms(x, g):
    ms = jnp.mean(x * x, axis=-1, keepdims=True)
    return x * lax.rsqrt(ms + EPS) * g


def _dot(a, b):
    return jnp.dot(a, b, preferred_element_type=F32)


def _fold_kernel(c_ref, s_ref, w_ref, o_ref):
    w = w_ref[0, 0]
    g = w.shape[-1]
    o_ref[0, 0, :, :g] = jnp.dot(c_ref[...], w, preferred_element_type=F32,
                                 precision=lax.Precision.HIGHEST)
    o_ref[0, 0, :, g:] = jnp.dot(s_ref[...], w, preferred_element_type=F32,
                                 precision=lax.Precision.HIGHEST)


def _fold_fnet(fnet_w):
    L, G, C, _ = fnet_w.shape
    idx = np.arange(C)
    ang = 2.0 * np.pi * ((idx[:, None] * idx[None, :]) % C) / C
    cmat = jnp.asarray((np.cos(ang) / math.sqrt(C)).astype(np.float32))
    smat = jnp.asarray((np.sin(ang) / math.sqrt(C)).astype(np.float32))
    return pl.pallas_call(
        _fold_kernel,
        grid=(L, G),
        in_specs=[pl.BlockSpec((C, C), lambda l, g: (0, 0)),
                  pl.BlockSpec((C, C), lambda l, g: (0, 0)),
                  pl.BlockSpec((1, 1, C, C), lambda l, g: (l, g, 0, 0))],
        out_specs=pl.BlockSpec((1, 1, C, 2 * C), lambda l, g: (l, g, 0, 0)),
        out_shape=jax.ShapeDtypeStruct((L, G, C, 2 * C), F32),
        compiler_params=_params(2),
        name="fold_fnet",
    )(cmat, smat, fnet_w)


def _inproj_kernel(h_ref, g_ref, win_ref, poolw_ref, wcs_ref, a_ref, v_ref, pq_ref):
    tile, d = h_ref.shape[1], h_ref.shape[2]
    gc = d // N_GROUPS
    xn = _rms(h_ref[0], g_ref[...]).astype(BF16)
    u_pool = _dot(xn, win_ref[:, 0:d]).astype(BF16)
    for gi in range(N_GROUPS):
        cs = slice(gi * gc, (gi + 1) * gc)
        a_ref[0, :, cs] = _dot(u_pool[:, cs], poolw_ref[gi]).astype(BF16)
    ca = _dot(xn, win_ref[:, d:2 * d])
    cg = _dot(xn, win_ref[:, 2 * d:3 * d])
    v = ca * jax.nn.sigmoid(cg)
    nj = d // LANE
    for j in range(nj):
        v_ref[0, pl.ds(j, tile, stride=nj), :] = v[:, j * LANE:(j + 1) * LANE]
    u_f = _dot(xn, win_ref[:, 3 * d:4 * d]).astype(BF16)
    for gi in range(N_GROUPS):
        cs = slice(gi * gc, (gi + 1) * gc)
        pq = _dot(u_f[:, cs], wcs_ref[gi])
        pq_ref[0, 0, :, cs] = pq[:, :gc].astype(BF16)
        pq_ref[0, 1, :, cs] = pq[:, gc:].astype(BF16)


def _inproj(h, g, win, poolw, wcs, tile):
    B, S, D = h.shape
    gc = D // N_GROUPS
    tok = lambda b, i: (b, i, 0)
    return pl.pallas_call(
        _inproj_kernel,
        grid=(B, S // tile),
        in_specs=[pl.BlockSpec((1, tile, D), tok),
                  _const_spec((1, D)),
                  _const_spec((D, 4 * D)),
                  _const_spec((N_GROUPS, gc, gc)),
                  _const_spec((N_GROUPS, gc, 2 * gc))],
        out_specs=[pl.BlockSpec((1, tile, D), tok),
                   pl.BlockSpec((1, tile * D // LANE, LANE), tok),
                   pl.BlockSpec((1, 2, tile, D), lambda b, i: (b, 0, i, 0))],
        out_shape=[jax.ShapeDtypeStruct((B, S, D), BF16),
                   jax.ShapeDtypeStruct((B, S * D // LANE, LANE), F32),
                   jax.ShapeDtypeStruct((B, 2, S, D), BF16)],
        compiler_params=_params(2),
        name="inproj",
    )(h, g, win, poolw, wcs)


def _regroup_matrix(n_outer, n_inner):
    n = n_outer * n_inner
    p = np.zeros((n, n), np.float32)
    for o in range(n_outer):
        for i in range(n_inner):
            p[i * n_outer + o, o * n_inner + i] = 1.0
    return p


def _dft1_kernel(g1_ref, pin_ref, pout_ref, x_ref, o_ref, *, rep):
    _, _, n1, sb, d = x_ref.shape
    rows = 2 * n1
    groups = rows // SUBLANE

    def block(ref, g):
        c, r0 = divmod(g * SUBLANE, n1)
        return ref.at[0, c, r0:r0 + SUBLANE]

    xp = [_dot(pin_ref[...], block(x_ref, g)[...].reshape(SUBLANE * sb, d)) for g in range(groups)]
    ts = []
    for q in range(sb // rep):
        xs = [xp[g][s * SUBLANE:(s + 1) * SUBLANE] for s in range(q * rep, (q + 1) * rep)
              for g in range(groups)]
        t = _dot(g1_ref[...], jnp.concatenate(xs, axis=0).astype(BF16))
        ts.extend(t[r * rows:(r + 1) * rows] for r in range(rep))
    for g in range(groups):
        st = jnp.concatenate([ts[s][g * SUBLANE:(g + 1) * SUBLANE] for s in range(sb)], axis=0)
        out = _dot(pout_ref[...], st.astype(BF16)).astype(BF16)
        block(o_ref, g)[...] = out.reshape(SUBLANE, sb, d)


def _dft2_kernel(g2_ref, p_ref, t_ref, o_ref):
    _, _, kb, n2, d = t_ref.shape
    res = []
    for kk in range(kb):
        x = jnp.concatenate([t_ref[0, 0, kk], t_ref[0, 1, kk]], axis=0)
        res.append(_dot(g2_ref[kk], x))
    for q in range(n2 // SUBLANE):
        st = jnp.concatenate([r[q * SUBLANE:(q + 1) * SUBLANE] for r in res], axis=0)
        out = _dot(p_ref[...], st.astype(BF16)).astype(BF16)
        o_ref[0, q * SUBLANE:(q + 1) * SUBLANE] = out.reshape(SUBLANE, kb, d)


def _dft_tables(S):
    n2 = DFT_N2
    n1 = S // n2
    rep = DFT_ROWS // (2 * n1)
    i1 = np.arange(n1)
    ang1 = 2.0 * np.pi * ((i1[:, None] * i1[None, :]) % n1) / n1
    c1, s1 = np.cos(ang1), np.sin(ang1)
    g1 = np.block([[c1, -s1], [s1, c1]])
    g1 = np.kron(np.eye(rep), g1).astype(np.float32)
    k = (i1[:, None] + n1 * np.arange(n2)[None, :])
    ang2 = 2.0 * np.pi * ((k[:, :, None] * np.arange(n2)[None, None, :]) % S) / S
    g2 = np.concatenate([np.cos(ang2), -np.sin(ang2)], axis=-1) / math.sqrt(S)
    return n1, rep, jnp.asarray(g1), jnp.asarray(g2.astype(np.float32))


def _seq_dft(pq, tables):
    B, _, S, D = pq.shape
    n1, rep, g1, g2 = tables
    n2 = DFT_N2
    sb = kb = DFT_BLOCK
    pin = jnp.asarray(_regroup_matrix(SUBLANE, sb)).astype(BF16)
    pout = jnp.asarray(_regroup_matrix(sb, SUBLANE)).astype(BF16)
    x = pq.reshape(B, 2, n1, n2, D)
    blk1 = pl.BlockSpec((1, 2, n1, sb, D), lambda b, j: (b, 0, 0, j, 0))
    t = pl.pallas_call(
        functools.partial(_dft1_kernel, rep=rep),
        grid=(B, n2 // sb),
        in_specs=[_const_spec((DFT_ROWS, DFT_ROWS)),
                  _const_spec((SUBLANE * sb, SUBLANE * sb)),
                  _const_spec((SUBLANE * sb, SUBLANE * sb)),
                  blk1],
        out_specs=blk1,
        out_shape=jax.ShapeDtypeStruct((B, 2, n1, n2, D), BF16),
        compiler_params=_params(2),
        name="dft_stage1",
    )(g1.astype(BF16), pin, pout, x)
    y = pl.pallas_call(
        _dft2_kernel,
        grid=(B, n1 // kb),
        in_specs=[pl.BlockSpec((kb, n2, 2 * n2), lambda b, j: (j, 0, 0)),
                  _const_spec((SUBLANE * kb, SUBLANE * kb)),
                  pl.BlockSpec((1, 2, kb, n2, D), lambda b, j: (b, 0, j, 0, 0))],
        out_specs=pl.BlockSpec((1, n2, kb, D), lambda b, j: (b, 0, j, 0)),
        out_shape=jax.ShapeDtypeStruct((B, n2, n1, D), BF16),
        compiler_params=_params(2),
        name="dft_stage2",
    )(g2.astype(BF16), pout, t)
    return y.reshape(B, S, D)


def _mix_kernel(h_ref, am_ref, ap_ref, an_ref, vm_ref, vp_ref, vn_ref, yc_ref,
                gmix_ref, wg_ref, bg_ref, pscale_ref, cw_ref, cb_ref, lng_ref, lnb_ref,
                cow_ref, wo_ref, o_ref, aext, vext, cbuf, *, seq_len):
    tile, d = h_ref.shape[1], h_ref.shape[2]
    gc = d // N_GROUPS
    i = pl.program_id(1)
    first = i == 0
    last = i == pl.num_programs(1) - 1

    aext[0:HALO] = jnp.where(first, 0.0, ap_ref[0].astype(F32))
    aext[HALO:HALO + tile] = am_ref[0].astype(F32)
    aext[HALO + tile:] = jnp.where(last, 0.0, an_ref[0].astype(F32))
    vext[0:HALO] = jnp.where(first, 0.0, vp_ref[0])
    vext[HALO:HALO + tile] = vm_ref[0]
    vext[HALO + tile:] = jnp.where(last, 0.0, vn_ref[0])

    h = h_ref[0]
    xn = _rms(h, gmix_ref[...]).astype(BF16)

    def gate(br):
        cs = slice(br * d, (br + 1) * d)
        return jax.nn.sigmoid(_dot(xn, wg_ref[:, cs]) + bg_ref[:, cs])

    rc = 8
    nj = d // LANE
    for c in range(tile // rc):
        base = c * rc + HALO - CONV_PAD
        acc = vext[base:base + rc] * cw_ref[0]
        for j in range(1, CONV_WIDTH):
            acc = acc + vext[base + j:base + j + rc] * cw_ref[j]
        cbuf[c * rc * nj:(c + 1) * rc * nj, :] = acc.reshape(rc * nj, LANE)
    cv = jnp.concatenate([cbuf[pl.ds(j, tile, stride=nj), :] for j in range(nj)], axis=-1)
    cv = cv + cb_ref[...]
    mu = jnp.mean(cv, axis=-1, keepdims=True)
    xc = cv - mu
    var = jnp.mean(xc * xc, axis=-1, keepdims=True)
    ln = xc * lax.rsqrt(var + EPS) * lng_ref[...] + lnb_ref[...]
    y_b = _dot((ln * jax.nn.sigmoid(ln)).astype(BF16), cow_ref[...])
    merged = gate(1) * y_b

    t = i * tile + lax.broadcasted_iota(jnp.int32, (tile, gc), 0)
    ya = []
    for gi, win in enumerate(POOL_WINDOWS):
        cs = slice(gi * gc, (gi + 1) * gc)
        half = win // 2
        acc = aext[HALO - half:HALO - half + tile, cs]
        for dlt in range(-half + 1, half):
            acc = acc + aext[HALO + dlt:HALO + dlt + tile, cs]
        lo = jnp.maximum(t - half, 0)
        hi = jnp.minimum(t + half - 1, seq_len - 1)
        cnt = (hi - lo + 1).astype(F32)
        ya.append(acc / cnt - aext[HALO:HALO + tile, cs])
    y_a = jnp.concatenate(ya, axis=-1) * pscale_ref[...]
    merged = merged + gate(0) * y_a
    merged = merged + gate(2) * yc_ref[0].astype(F32)

    o_ref[0] = h + _dot(merged.astype(BF16), wo_ref[...])


def _mix(h, a, v, yc, gmix, wg, bg, pscale, cw, cb, lng, lnb, cow, wo, tile):
    B, S, D = h.shape
    nh = tile // HALO
    n_halo_blocks = S // HALO
    tok = lambda b, i: (b, i, 0)
    prev = lambda b, i: (b, jnp.maximum(i * nh - 1, 0), 0)
    nxt = lambda b, i: (b, jnp.minimum((i + 1) * nh, n_halo_blocks - 1), 0)
    main_spec = pl.BlockSpec((1, tile, D), tok)
    prev_spec = pl.BlockSpec((1, HALO, D), prev)
    next_spec = pl.BlockSpec((1, HALO, D), nxt)
    nj = D // LANE
    v = v.reshape(B, S, nj, LANE)
    vmain_spec = pl.BlockSpec((1, tile, nj, LANE), lambda b, i: tok(b, i) + (0,))
    vprev_spec = pl.BlockSpec((1, HALO, nj, LANE), lambda b, i: prev(b, i) + (0,))
    vnext_spec = pl.BlockSpec((1, HALO, nj, LANE), lambda b, i: nxt(b, i) + (0,))
    cw = cw.reshape(cw.shape[0], nj, LANE)
    return pl.pallas_call(
        functools.partial(_mix_kernel, seq_len=S),
        grid=(B, S // tile),
        in_specs=[main_spec,
                  main_spec, prev_spec, next_spec,
                  vmain_spec, vprev_spec, vnext_spec,
                  main_spec,
                  _const_spec((1, D)),
                  _const_spec((D, N_BRANCH * D)),
                  _const_spec((1, N_BRANCH * D)),
                  _const_spec((1, D)),
                  _const_spec((CONV_WIDTH + 1, nj, LANE)),
                  _const_spec((1, D)),
                  _const_spec((1, D)),
                  _const_spec((1, D)),
                  _const_spec((D, D)),
                  _const_spec((D, D))],
        out_specs=main_spec,
        out_shape=jax.ShapeDtypeStruct((B, S, D), F32),
        scratch_shapes=[pltpu.VMEM((tile + 2 * HALO, D), F32),
                        pltpu.VMEM((tile + 2 * HALO, nj, LANE), F32),
                        pltpu.VMEM((tile * nj, LANE), F32)],
        compiler_params=_params(2),
        name="mix",
    )(h, a, a, a, v, v, v, yc, gmix, wg, bg, pscale, cw, cb, lng, lnb, cow, wo)


def _ffn_kernel(h_ref, p_ref, gff_ref, w1_ref, w2_ref, gple_ref, wpg_ref, wpp_ref, gfin_ref,
                o_ref, *, final):
    h = h_ref[0]
    xn = _rms(h, gff_ref[...]).astype(BF16)
    hf = jnp.maximum(_dot(xn, w1_ref[...]), 0.0)
    h = h + _dot((hf * hf).astype(BF16), w2_ref[...])
    pg = jax.nn.sigmoid(_dot(_rms(h, gple_ref[...]).astype(BF16), wpg_ref[...]))
    h = h + _dot(p_ref[0, 0].astype(BF16), wpp_ref[...]) * pg
    if final:
        h = _rms(h, gfin_ref[...])
    o_ref[0] = h


def _ffn(h, p, layer, gff, w1, w2, gple, wpg, wpp, gfin, tile, final):
    B, S, D = h.shape
    dff = w1.shape[-1]
    dple = p.shape[-1]
    tok = lambda b, i: (b, i, 0)
    return pl.pallas_call(
        functools.partial(_ffn_kernel, final=final),
        grid=(B, S // tile),
        in_specs=[pl.BlockSpec((1, tile, D), tok),
                  pl.BlockSpec((1, 1, tile, dple), lambda b, i: (layer, b, i, 0)),
                  _const_spec((1, D)),
                  _const_spec((D, dff)),
                  _const_spec((dff, D)),
                  _const_spec((1, D)),
                  _const_spec((D, D)),
                  _const_spec((dple, D)),
                  _const_spec((1, D))],
        out_specs=pl.BlockSpec((1, tile, D), tok),
        out_shape=jax.ShapeDtypeStruct((B, S, D), F32),
        compiler_params=_params(2),
        name="ffn_ple",
    )(h, p, gff, w1, w2, gple, wpg, wpp, gfin)


def _trunk(x, p, w, tables):
    depth = w["w_in_a"].shape[0]
    h = x
    for l in range(depth):
        a, v, pq = _inproj(h, w["norm_mix_g"][l], w["w_in_a"][l], w["pool_w"][l], w["wcs"][l], tile=512)
        yc = _seq_dft(pq, tables)
        h = _mix(h, a, v, yc, w["norm_mix_g"][l], w["w_in_g"][l], w["b_gate"][l], w["pool_scale"][l],
                 w["conv_w"][l], w["conv_b"][l], w["conv_ln_g"][l], w["conv_ln_b"][l],
                 w["conv_out_w"][l], w["w_o"][l], tile=256)
        h = _ffn(h, p, l, w["norm_ff_g"][l], w["w_ff1"][l], w["w_ff2"][l], w["norm_ple_g"][l],
                 w["w_ple_gate"][l], w["w_ple_proj"][l], w["final_norm_g"], tile=512,
                 final=(l == depth - 1))
    return h


def kernel(x_prompt, x_sample, p_prompt, p_sample, norm_mix_g, w_in, b_gate, pool_w, pool_scale,
           conv_w, conv_b, conv_ln_g, conv_ln_b, conv_out_w, fnet_w, w_o, norm_ff_g, w_ff1, w_ff2,
           norm_ple_g, w_ple_gate, w_ple_proj, final_norm_g):
    L, D, _ = w_in.shape
    row = lambda t: t.reshape(t.shape[0], 1, t.shape[-1])
    w = {
        "norm_mix_g": row(norm_mix_g),
        "w_in_a": w_in[:, :, :4 * D].astype(BF16),
        "w_in_g": w_in[:, :, 4 * D:].astype(BF16),
        "b_gate": row(b_gate),
        "pool_w": pool_w.astype(BF16),
        "pool_scale": row(pool_scale),
        "conv_w": jnp.pad(conv_w, ((0, 0), (0, 1), (0, 0))),
        "conv_b": row(conv_b),
        "conv_ln_g": row(conv_ln_g),
        "conv_ln_b": row(conv_ln_b),
        "conv_out_w": conv_out_w.astype(BF16),
        "wcs": _fold_fnet(fnet_w).astype(BF16),
        "w_o": w_o.astype(BF16),
        "norm_ff_g": row(norm_ff_g),
        "w_ff1": w_ff1.astype(BF16),
        "w_ff2": w_ff2.astype(BF16),
        "norm_ple_g": row(norm_ple_g),
        "w_ple_gate": w_ple_gate.astype(BF16),
        "w_ple_proj": w_ple_proj.astype(BF16),
        "final_norm_g": final_norm_g.reshape(1, D),
    }
    y_prompt = _trunk(x_prompt, p_prompt, w, _dft_tables(x_prompt.shape[1]))
    y_sample = _trunk(x_sample, p_sample, w, _dft_tables(x_sample.shape[1]))
    return (y_prompt, y_sample)
```

```python
import functools
import math

import numpy as np
import jax
import jax.numpy as jnp
from jax import lax
from jax.experimental import pallas as pl
from jax.experimental.pallas import tpu as pltpu

N_GROUPS = 4
POOL_WINDOWS = (2, 4, 8, 16)
CONV_WIDTH = 31
CONV_PAD = CONV_WIDTH // 2
N_BRANCH = 3
EPS = 1e-6

LANE = 128
SUBLANE = 8
HALO = 16
DFT_N2 = 128
DFT_ROWS = 128
DFT_BLOCK = 16
VMEM_LIMIT_BYTES = 56 * 1024 * 1024

F32 = jnp.float32
BF16 = jnp.bfloat16


def _const_spec(shape):
    nd = len(shape)
    return pl.BlockSpec(shape, lambda *_: (0,) * nd, pipeline_mode=pl.Buffered(1))


def _params(n_grid):
    return pltpu.CompilerParams(dimension_semantics=("parallel",) * n_grid,
                                vmem_limit_bytes=VMEM_LIMIT_BYTES)


def _rms(x, g):
    ms = jnp.mean(x * x, axis=-1, keepdims=True)
    return x * lax.rsqrt(ms + EPS) * g


def _dot(a, b):
    return jnp.dot(a, b, preferred_element_type=F32)


def _fold_kernel(c_ref, s_ref, w_ref, o_ref):
    w = w_ref[0, 0]
    g = w.shape[-1]
    o_ref[0, 0, :, :g] = jnp.dot(c_ref[...], w, preferred_element_type=F32,
                                 precision=lax.Precision.HIGHEST)
    o_ref[0, 0, :, g:] = jnp.dot(s_ref[...], w, preferred_element_type=F32,
                                 precision=lax.Precision.HIGHEST)


def _fold_fnet(fnet_w):
    L, G, C, _ = fnet_w.shape
    idx = np.arange(C)
    ang = 2.0 * np.pi * ((idx[:, None] * idx[None, :]) % C) / C
    cmat = jnp.asarray((np.cos(ang) / math.sqrt(C)).astype(np.float32))
    smat = jnp.asarray((np.sin(ang) / math.sqrt(C)).astype(np.float32))
    return pl.pallas_call(
        _fold_kernel,
        grid=(L, G),
        in_specs=[pl.BlockSpec((C, C), lambda l, g: (0, 0)),
                  pl.BlockSpec((C, C), lambda l, g: (0, 0)),
                  pl.BlockSpec((1, 1, C, C), lambda l, g: (l, g, 0, 0))],
        out_specs=pl.BlockSpec((1, 1, C, 2 * C), lambda l, g: (l, g, 0, 0)),
        out_shape=jax.ShapeDtypeStruct((L, G, C, 2 * C), F32),
        compiler_params=_params(2),
        name="fold_fnet",
    )(cmat, smat, fnet_w)


def _inproj_kernel(h_ref, g_ref, win_ref, poolw_ref, wcs_ref, a_ref, v_ref, pq_ref, xn_ref):
    tile, d = h_ref.shape[1], h_ref.shape[2]
    gc = d // N_GROUPS
    xn = _rms(h_ref[0], g_ref[...]).astype(BF16)
    xn_ref[0] = xn
    u_pool = _dot(xn, win_ref[:, 0:d]).astype(BF16)
    for gi in range(N_GROUPS):
        cs = slice(gi * gc, (gi + 1) * gc)
        a_ref[0, :, cs] = _dot(u_pool[:, cs], poolw_ref[gi]).astype(BF16)
    ca = _dot(xn, win_ref[:, d:2 * d])
    cg = _dot(xn, win_ref[:, 2 * d:3 * d])
    v = ca * jax.nn.sigmoid(cg)
    nj = d // LANE
    for j in range(nj):
        v_ref[0, pl.ds(j, tile, stride=nj), :] = v[:, j * LANE:(j + 1) * LANE]
    u_f = _dot(xn, win_ref[:, 3 * d:4 * d]).astype(BF16)
    for gi in range(N_GROUPS):
        cs = slice(gi * gc, (gi + 1) * gc)
        pq = _dot(u_f[:, cs], wcs_ref[gi])
        pq_ref[0, 0, :, cs] = pq[:, :gc].astype(BF16)
        pq_ref[0, 1, :, cs] = pq[:, gc:].astype(BF16)


def _inproj(h, g, win, poolw, wcs, tile):
    B, S, D = h.shape
    gc = D // N_GROUPS
    tok = lambda b, i: (b, i, 0)
    return pl.pallas_call(
        _inproj_kernel,
        grid=(B, S // tile),
        in_specs=[pl.BlockSpec((1, tile, D), tok),
                  _const_spec((1, D)),
                  _const_spec((D, 4 * D)),
                  _const_spec((N_GROUPS, gc, gc)),
                  _const_spec((N_GROUPS, gc, 2 * gc))],
        out_specs=[pl.BlockSpec((1, tile, D), tok),
                   pl.BlockSpec((1, tile * D // LANE, LANE), tok),
                   pl.BlockSpec((1, 2, tile, D), lambda b, i: (b, 0, i, 0)),
                   pl.BlockSpec((1, tile, D), tok)],
        out_shape=[jax.ShapeDtypeStruct((B, S, D), BF16),
                   jax.ShapeDtypeStruct((B, S * D // LANE, LANE), F32),
                   jax.ShapeDtypeStruct((B, 2, S, D), BF16),
                   jax.ShapeDtypeStruct((B, S, D), BF16)],
        compiler_params=_params(2),
        name="inproj",
    )(h, g, win, poolw, wcs)


def _regroup_matrix(n_outer, n_inner):
    n = n_outer * n_inner
    p = np.zeros((n, n), np.float32)
    for o in range(n_outer):
        for i in range(n_inner):
            p[i * n_outer + o, o * n_inner + i] = 1.0
    return p


def _dft1_kernel(g1_ref, pin_ref, pout_ref, x_ref, o_ref, *, rep):
    _, _, n1, sb, d = x_ref.shape
    rows = 2 * n1
    groups = rows // SUBLANE

    def block(ref, g):
        c, r0 = divmod(g * SUBLANE, n1)
        return ref.at[0, c, r0:r0 + SUBLANE]

    xp = [_dot(pin_ref[...], block(x_ref, g)[...].reshape(SUBLANE * sb, d)) for g in range(groups)]
    ts = []
    for q in range(sb // rep):
        xs = [xp[g][s * SUBLANE:(s + 1) * SUBLANE] for s in range(q * rep, (q + 1) * rep)
              for g in range(groups)]
        t = _dot(g1_ref[...], jnp.concatenate(xs, axis=0).astype(BF16))
        ts.extend(t[r * rows:(r + 1) * rows] for r in range(rep))
    for g in range(groups):
        st = jnp.concatenate([ts[s][g * SUBLANE:(g + 1) * SUBLANE] for s in range(sb)], axis=0)
        out = _dot(pout_ref[...], st.astype(BF16)).astype(BF16)
        block(o_ref, g)[...] = out.reshape(SUBLANE, sb, d)


def _dft2_kernel(g2_ref, p_ref, t_ref, o_ref):
    _, _, kb, n2, d = t_ref.shape
    res = []
    for kk in range(kb):
        x = jnp.concatenate([t_ref[0, 0, kk], t_ref[0, 1, kk]], axis=0)
        res.append(_dot(g2_ref[kk], x))
    for q in range(n2 // SUBLANE):
        st = jnp.concatenate([r[q * SUBLANE:(q + 1) * SUBLANE] for r in res], axis=0)
        out = _dot(p_ref[...], st.astype(BF16)).astype(BF16)
        o_ref[0, q * SUBLANE:(q + 1) * SUBLANE] = out.reshape(SUBLANE, kb, d)


def _dft_tables(S):
    n2 = DFT_N2
    n1 = S // n2
    rep = DFT_ROWS // (2 * n1)
    i1 = np.arange(n1)
    ang1 = 2.0 * np.pi * ((i1[:, None] * i1[None, :]) % n1) / n1
    c1, s1 = np.cos(ang1), np.sin(ang1)
    g1 = np.block([[c1, -s1], [s1, c1]])
    g1 = np.kron(np.eye(rep), g1).astype(np.float32)
    k = (i1[:, None] + n1 * np.arange(n2)[None, :])
    ang2 = 2.0 * np.pi * ((k[:, :, None] * np.arange(n2)[None, None, :]) % S) / S
    g2 = np.concatenate([np.cos(ang2), -np.sin(ang2)], axis=-1) / math.sqrt(S)
    return n1, rep, jnp.asarray(g1), jnp.asarray(g2.astype(np.float32))


def _seq_dft(pq, tables):
    B, _, S, D = pq.shape
    n1, rep, g1, g2 = tables
    n2 = DFT_N2
    sb = kb = DFT_BLOCK
    pin = jnp.asarray(_regroup_matrix(SUBLANE, sb)).astype(BF16)
    pout = jnp.asarray(_regroup_matrix(sb, SUBLANE)).astype(BF16)
    x = pq.reshape(B, 2, n1, n2, D)
    blk1 = pl.BlockSpec((1, 2, n1, sb, D), lambda b, j: (b, 0, 0, j, 0))
    t = pl.pallas_call(
        functools.partial(_dft1_kernel, rep=rep),
        grid=(B, n2 // sb),
        in_specs=[_const_spec((DFT_ROWS, DFT_ROWS)),
                  _const_spec((SUBLANE * sb, SUBLANE * sb)),
                  _const_spec((SUBLANE * sb, SUBLANE * sb)),
                  blk1],
        out_specs=blk1,
        out_shape=jax.ShapeDtypeStruct((B, 2, n1, n2, D), BF16),
        compiler_params=_params(2),
        name="dft_stage1",
    )(g1.astype(BF16), pin, pout, x)
    y = pl.pallas_call(
        _dft2_kernel,
        grid=(B, n1 // kb),
        in_specs=[pl.BlockSpec((kb, n2, 2 * n2), lambda b, j: (j, 0, 0)),
                  _const_spec((SUBLANE * kb, SUBLANE * kb)),
                  pl.BlockSpec((1, 2, kb, n2, D), lambda b, j: (b, 0, j, 0, 0))],
        out_specs=pl.BlockSpec((1, n2, kb, D), lambda b, j: (b, 0, j, 0)),
        out_shape=jax.ShapeDtypeStruct((B, n2, n1, D), BF16),
        compiler_params=_params(2),
        name="dft_stage2",
    )(g2.astype(BF16), pout, t)
    return y.reshape(B, S, D)


def _mix_kernel(h_ref, xn_ref, am_ref, ap_ref, an_ref, vm_ref, vp_ref, vn_ref, yc_ref,
                band_ref, wg_ref, bg_ref, pscale_ref, cw_ref, cb_ref, lng_ref, lnb_ref,
                cow_ref, wo_ref, o_ref, vext, cbuf, *, seq_len):
    tile, d = h_ref.shape[1], h_ref.shape[2]
    gc = d // N_GROUPS
    i = pl.program_id(1)
    first = i == 0
    last = i == pl.num_programs(1) - 1

    vext[0:HALO] = jnp.where(first, 0.0, vp_ref[0])
    vext[HALO:HALO + tile] = vm_ref[0]
    vext[HALO + tile:] = jnp.where(last, 0.0, vn_ref[0])

    h = h_ref[0]
    xn = xn_ref[0]

    def gate(br):
        cs = slice(br * d, (br + 1) * d)
        return jax.nn.sigmoid(_dot(xn, wg_ref[:, cs]) + bg_ref[:, cs])

    rc = 8
    nj = d // LANE
    for c in range(tile // rc):
        base = c * rc + HALO - CONV_PAD
        acc = vext[base:base + rc] * cw_ref[0]
        for j in range(1, CONV_WIDTH):
            acc = acc + vext[base + j:base + j + rc] * cw_ref[j]
        cbuf[c * rc * nj:(c + 1) * rc * nj, :] = acc.reshape(rc * nj, LANE)
    cv = jnp.concatenate([cbuf[pl.ds(j, tile, stride=nj), :] for j in range(nj)], axis=-1)
    cv = cv + cb_ref[...]
    mu = jnp.mean(cv, axis=-1, keepdims=True)
    xc = cv - mu
    var = jnp.mean(xc * xc, axis=-1, keepdims=True)
    ln = xc * lax.rsqrt(var + EPS) * lng_ref[...] + lnb_ref[...]
    y_b = _dot((ln * jax.nn.sigmoid(ln)).astype(BF16), cow_ref[...])
    merged = gate(1) * y_b

    a_main = am_ref[0]
    a_prev = ap_ref[0]
    a_next = an_ref[0]
    pad = jnp.zeros((band_ref.shape[-1] - tile - 2 * HALO, d), BF16)
    a_ext = jnp.concatenate([a_main,
                             jnp.where(first, jnp.zeros_like(a_prev), a_prev),
                             jnp.where(last, jnp.zeros_like(a_next), a_next),
                             pad], axis=0)
    a_f32 = a_main.astype(F32)
    t = i * tile + lax.broadcasted_iota(jnp.int32, (tile, gc), 0)
    ya = []
    for gi, win in enumerate(POOL_WINDOWS):
        cs = slice(gi * gc, (gi + 1) * gc)
        half = win // 2
        acc = _dot(band_ref[gi], a_ext[:, cs])
        lo = jnp.maximum(t - half, 0)
        hi = jnp.minimum(t + half - 1, seq_len - 1)
        cnt = (hi - lo + 1).astype(F32)
        ya.append(acc / cnt - a_f32[:, cs])
    y_a = jnp.concatenate(ya, axis=-1) * pscale_ref[...]
    merged = merged + gate(0) * y_a
    merged = merged + gate(2) * yc_ref[0].astype(F32)

    o_ref[0] = h + _dot(merged.astype(BF16), wo_ref[...])


def _pool_bands(tile):
    width = -(-(tile + 2 * HALO) // LANE) * LANE
    bands = np.zeros((len(POOL_WINDOWS), tile, width), np.float32)
    for gi, win in enumerate(POOL_WINDOWS):
        for t in range(tile):
            for src in range(t - win // 2, t + win // 2):
                if src < 0:
                    col = tile + HALO + src
                elif src >= tile:
                    col = tile + HALO + (src - tile)
                else:
                    col = src
                bands[gi, t, col] = 1.0
    return bands


def _mix(h, xn, a, v, yc, wg, bg, pscale, cw, cb, lng, lnb, cow, wo, tile):
    B, S, D = h.shape
    bands = jnp.asarray(_pool_bands(tile)).astype(BF16)
    nh = tile // HALO
    n_halo_blocks = S // HALO
    tok = lambda b, i: (b, i, 0)
    prev = lambda b, i: (b, jnp.maximum(i * nh - 1, 0), 0)
    nxt = lambda b, i: (b, jnp.minimum((i + 1) * nh, n_halo_blocks - 1), 0)
    main_spec = pl.BlockSpec((1, tile, D), tok)
    prev_spec = pl.BlockSpec((1, HALO, D), prev)
    next_spec = pl.BlockSpec((1, HALO, D), nxt)
    nj = D // LANE
    v = v.reshape(B, S, nj, LANE)
    vmain_spec = pl.BlockSpec((1, tile, nj, LANE), lambda b, i: tok(b, i) + (0,))
    vprev_spec = pl.BlockSpec((1, HALO, nj, LANE), lambda b, i: prev(b, i) + (0,))
    vnext_spec = pl.BlockSpec((1, HALO, nj, LANE), lambda b, i: nxt(b, i) + (0,))
    cw = cw.reshape(cw.shape[0], nj, LANE)
    return pl.pallas_call(
        functools.partial(_mix_kernel, seq_len=S),
        grid=(B, S // tile),
        in_specs=[main_spec,
                  main_spec,
                  main_spec, prev_spec, next_spec,
                  vmain_spec, vprev_spec, vnext_spec,
                  main_spec,
                  _const_spec(bands.shape),
                  _const_spec((D, N_BRANCH * D)),
                  _const_spec((1, N_BRANCH * D)),
                  _const_spec((1, D)),
                  _const_spec((CONV_WIDTH + 1, nj, LANE)),
                  _const_spec((1, D)),
                  _const_spec((1, D)),
                  _const_spec((1, D)),
                  _const_spec((D, D)),
                  _const_spec((D, D))],
        out_specs=main_spec,
        out_shape=jax.ShapeDtypeStruct((B, S, D), F32),
        scratch_shapes=[pltpu.VMEM((tile + 2 * HALO, nj, LANE), F32),
                        pltpu.VMEM((tile * nj, LANE), F32)],
        compiler_params=_params(2),
        name="mix",
    )(h, xn, a, a, a, v, v, v, yc, bands, wg, bg, pscale, cw, cb, lng, lnb, cow, wo)


def _ffn_kernel(h_ref, p_ref, gff_ref, w1_ref, w2_ref, gple_ref, wpg_ref, wpp_ref, gfin_ref,
                o_ref, *, final):
    h = h_ref[0]
    xn = _rms(h, gff_ref[...]).astype(BF16)
    hf = jnp.maximum(_dot(xn, w1_ref[...]), 0.0)
    h = h + _dot((hf * hf).astype(BF16), w2_ref[...])
    pg = jax.nn.sigmoid(_dot(_rms(h, gple_ref[...]).astype(BF16), wpg_ref[...]))
    h = h + _dot(p_ref[0, 0].astype(BF16), wpp_ref[...]) * pg
    if final:
        h = _rms(h, gfin_ref[...])
    o_ref[0] = h


def _ffn(h, p, layer, gff, w1, w2, gple, wpg, wpp, gfin, tile, final):
    B, S, D = h.shape
    dff = w1.shape[-1]
    dple = p.shape[-1]
    tok = lambda b, i: (b, i, 0)
    return pl.pallas_call(
        functools.partial(_ffn_kernel, final=final),
        grid=(B, S // tile),
        in_specs=[pl.BlockSpec((1, tile, D), tok),
                  pl.BlockSpec((1, 1, tile, dple), lambda b, i: (layer, b, i, 0)),
                  _const_spec((1, D)),
                  _const_spec((D, dff)),
                  _const_spec((dff, D)),
                  _const_spec((1, D)),
                  _const_spec((D, D)),
                  _const_spec((dple, D)),
                  _const_spec((1, D))],
        out_specs=pl.BlockSpec((1, tile, D), tok),
        out_shape=jax.ShapeDtypeStruct((B, S, D), F32),
        compiler_params=_params(2),
        name="ffn_ple",
    )(h, p, gff, w1, w2, gple, wpg, wpp, gfin)


def _trunk(x, p, w, tables):
    depth = w["w_in_a"].shape[0]
    h = x
    for l in range(depth):
        a, v, pq, xn = _inproj(h, w["norm_mix_g"][l], w["w_in_a"][l], w["pool_w"][l], w["wcs"][l], tile=512)
        yc = _seq_dft(pq, tables)
        h = _mix(h, xn, a, v, yc, w["w_in_g"][l], w["b_gate"][l], w["pool_scale"][l],
                 w["conv_w"][l], w["conv_b"][l], w["conv_ln_g"][l], w["conv_ln_b"][l],
                 w["conv_out_w"][l], w["w_o"][l], tile=256)
        h = _ffn(h, p, l, w["norm_ff_g"][l], w["w_ff1"][l], w["w_ff2"][l], w["norm_ple_g"][l],
                 w["w_ple_gate"][l], w["w_ple_proj"][l], w["final_norm_g"], tile=512,
                 final=(l == depth - 1))
    return h


def kernel(x_prompt, x_sample, p_prompt, p_sample, norm_mix_g, w_in, b_gate, pool_w, pool_scale,
           conv_w, conv_b, conv_ln_g, conv_ln_b, conv_out_w, fnet_w, w_o, norm_ff_g, w_ff1, w_ff2,
           norm_ple_g, w_ple_gate, w_ple_proj, final_norm_g):
    L, D, _ = w_in.shape
    row = lambda t: t.reshape(t.shape[0], 1, t.shape[-1])
    w = {
        "norm_mix_g": row(norm_mix_g),
        "w_in_a": w_in[:, :, :4 * D].astype(BF16),
        "w_in_g": w_in[:, :, 4 * D:].astype(BF16),
        "b_gate": row(b_gate),
        "pool_w": pool_w.astype(BF16),
        "pool_scale": row(pool_scale),
        "conv_w": jnp.pad(conv_w, ((0, 0), (0, 1), (0, 0))),
        "conv_b": row(conv_b),
        "conv_ln_g": row(conv_ln_g),
        "conv_ln_b": row(conv_ln_b),
        "conv_out_w": conv_out_w.astype(BF16),
        "wcs": _fold_fnet(fnet_w).astype(BF16),
        "w_o": w_o.astype(BF16),
        "norm_ff_g": row(norm_ff_g),
        "w_ff1": w_ff1.astype(BF16),
        "w_ff2": w_ff2.astype(BF16),
        "norm_ple_g": row(norm_ple_g),
        "w_ple_gate": w_ple_gate.astype(BF16),
        "w_ple_proj": w_ple_proj.astype(BF16),
        "final_norm_g": final_norm_g.reshape(1, D),
    }
    y_prompt = _trunk(x_prompt, p_prompt, w, _dft_tables(x_prompt.shape[1]))
    y_sample = _trunk(x_sample, p_sample, w, _dft_tables(x_sample.shape[1]))
    return (y_prompt, y_sample)
```

```python
import functools
import math

import numpy as np
import jax
import jax.numpy as jnp
from jax import lax
from jax.experimental import pallas as pl
from jax.experimental.pallas import tpu as pltpu

N_GROUPS = 4
POOL_WINDOWS = (2, 4, 8, 16)
CONV_WIDTH = 31
CONV_PAD = CONV_WIDTH // 2
N_BRANCH = 3
EPS = 1e-6

LANE = 128
SUBLANE = 8
HALO = 16
DFT_N2 = 128
DFT_ROWS = 128
DFT_BLOCK = 16
VMEM_LIMIT_BYTES = 56 * 1024 * 1024

F32 = jnp.float32
BF16 = jnp.bfloat16


def _const_spec(shape):
    nd = len(shape)
    return pl.BlockSpec(shape, lambda *_: (0,) * nd, pipeline_mode=pl.Buffered(1))


def _params(n_grid):
    return pltpu.CompilerParams(dimension_semantics=("parallel",) * n_grid,
                                vmem_limit_bytes=VMEM_LIMIT_BYTES)


def _rms(x, g):
    ms = jnp.mean(x * x, axis=-1, keepdims=True)
    return x * lax.rsqrt(ms + EPS) * g


def _dot(a, b):
    return jnp.dot(a, b, preferred_element_type=F32)


def _fold_kernel(c_ref, s_ref, w_ref, o_ref):
    w = w_ref[0, 0]
    g = w.shape[-1]
    o_ref[0, 0, :, :g] = jnp.dot(c_ref[...], w, preferred_element_type=F32,
                                 precision=lax.Precision.HIGHEST)
    o_ref[0, 0, :, g:] = jnp.dot(s_ref[...], w, preferred_element_type=F32,
                                 precision=lax.Precision.HIGHEST)


def _fold_fnet(fnet_w):
    L, G, C, _ = fnet_w.shape
    idx = np.arange(C)
    ang = 2.0 * np.pi * ((idx[:, None] * idx[None, :]) % C) / C
    cmat = jnp.asarray((np.cos(ang) / math.sqrt(C)).astype(np.float32))
    smat = jnp.asarray((np.sin(ang) / math.sqrt(C)).astype(np.float32))
    return pl.pallas_call(
        _fold_kernel,
        grid=(L, G),
        in_specs=[pl.BlockSpec((C, C), lambda l, g: (0, 0)),
                  pl.BlockSpec((C, C), lambda l, g: (0, 0)),
                  pl.BlockSpec((1, 1, C, C), lambda l, g: (l, g, 0, 0))],
        out_specs=pl.BlockSpec((1, 1, C, 2 * C), lambda l, g: (l, g, 0, 0)),
        out_shape=jax.ShapeDtypeStruct((L, G, C, 2 * C), F32),
        compiler_params=_params(2),
        name="fold_fnet",
    )(cmat, smat, fnet_w)


def _inproj_kernel(h_ref, g_ref, win_ref, poolw_ref, wcs_ref, a_ref, v_ref, pq_ref, xn_ref):
    tile, d = h_ref.shape[1], h_ref.shape[2]
    gc = d // N_GROUPS
    xn = _rms(h_ref[0], g_ref[...]).astype(BF16)
    xn_ref[0] = xn
    u_pool = _dot(xn, win_ref[:, 0:d]).astype(BF16)
    for gi in range(N_GROUPS):
        cs = slice(gi * gc, (gi + 1) * gc)
        a_ref[0, :, cs] = _dot(u_pool[:, cs], poolw_ref[gi]).astype(BF16)
    ca = _dot(xn, win_ref[:, d:2 * d])
    cg = _dot(xn, win_ref[:, 2 * d:3 * d])
    v = ca * jax.nn.sigmoid(cg)
    nj = d // LANE
    for j in range(nj):
        v_ref[0, pl.ds(j, tile, stride=nj), :] = v[:, j * LANE:(j + 1) * LANE]
    u_f = _dot(xn, win_ref[:, 3 * d:4 * d]).astype(BF16)
    for gi in range(N_GROUPS):
        cs = slice(gi * gc, (gi + 1) * gc)
        pq = _dot(u_f[:, cs], wcs_ref[gi])
        pq_ref[0, 0, :, cs] = pq[:, :gc].astype(BF16)
        pq_ref[0, 1, :, cs] = pq[:, gc:].astype(BF16)


def _inproj(h, g, win, poolw, wcs, tile):
    B, S, D = h.shape
    gc = D // N_GROUPS
    tok = lambda b, i: (b, i, 0)
    return pl.pallas_call(
        _inproj_kernel,
        grid=(B, S // tile),
        in_specs=[pl.BlockSpec((1, tile, D), tok),
                  _const_spec((1, D)),
                  _const_spec((D, 4 * D)),
                  _const_spec((N_GROUPS, gc, gc)),
                  _const_spec((N_GROUPS, gc, 2 * gc))],
        out_specs=[pl.BlockSpec((1, tile, D), tok),
                   pl.BlockSpec((1, tile * D // LANE, LANE), tok),
                   pl.BlockSpec((1, 2, tile, D), lambda b, i: (b, 0, i, 0)),
                   pl.BlockSpec((1, tile, D), tok)],
        out_shape=[jax.ShapeDtypeStruct((B, S, D), BF16),
                   jax.ShapeDtypeStruct((B, S * D // LANE, LANE), F32),
                   jax.ShapeDtypeStruct((B, 2, S, D), BF16),
                   jax.ShapeDtypeStruct((B, S, D), BF16)],
        compiler_params=_params(2),
        name="inproj",
    )(h, g, win, poolw, wcs)


def _regroup_matrix(n_outer, n_inner):
    n = n_outer * n_inner
    p = np.zeros((n, n), np.float32)
    for o in range(n_outer):
        for i in range(n_inner):
            p[i * n_outer + o, o * n_inner + i] = 1.0
    return p


def _dft1_kernel(g1_ref, pin_ref, pout_ref, x_ref, o_ref, *, rep):
    _, _, n1, sb, d = x_ref.shape
    rows = 2 * n1
    groups = rows // SUBLANE

    def block(ref, g):
        c, r0 = divmod(g * SUBLANE, n1)
        return ref.at[0, c, r0:r0 + SUBLANE]

    xp = [_dot(pin_ref[...], block(x_ref, g)[...].reshape(SUBLANE * sb, d)) for g in range(groups)]
    ts = []
    for q in range(sb // rep):
        xs = [xp[g][s * SUBLANE:(s + 1) * SUBLANE] for s in range(q * rep, (q + 1) * rep)
              for g in range(groups)]
        t = _dot(g1_ref[...], jnp.concatenate(xs, axis=0).astype(BF16))
        ts.extend(t[r * rows:(r + 1) * rows] for r in range(rep))
    for g in range(groups):
        st = jnp.concatenate([ts[s][g * SUBLANE:(g + 1) * SUBLANE] for s in range(sb)], axis=0)
        out = _dot(pout_ref[...], st.astype(BF16)).astype(BF16)
        block(o_ref, g)[...] = out.reshape(SUBLANE, sb, d)


def _dft2_kernel(g2_ref, p_ref, t_ref, o_ref):
    _, _, kb, n2, d = t_ref.shape
    res = []
    for kk in range(kb):
        x = jnp.concatenate([t_ref[0, 0, kk], t_ref[0, 1, kk]], axis=0)
        res.append(_dot(g2_ref[kk], x))
    for q in range(n2 // SUBLANE):
        st = jnp.concatenate([r[q * SUBLANE:(q + 1) * SUBLANE] for r in res], axis=0)
        out = _dot(p_ref[...], st.astype(BF16)).astype(BF16)
        o_ref[0, q * SUBLANE:(q + 1) * SUBLANE] = out.reshape(SUBLANE, kb, d)


def _dft_tables(S):
    n2 = DFT_N2
    n1 = S // n2
    rep = DFT_ROWS // (2 * n1)
    i1 = np.arange(n1)
    ang1 = 2.0 * np.pi * ((i1[:, None] * i1[None, :]) % n1) / n1
    c1, s1 = np.cos(ang1), np.sin(ang1)
    g1 = np.block([[c1, -s1], [s1, c1]])
    g1 = np.kron(np.eye(rep), g1).astype(np.float32)
    k = (i1[:, None] + n1 * np.arange(n2)[None, :])
    ang2 = 2.0 * np.pi * ((k[:, :, None] * np.arange(n2)[None, None, :]) % S) / S
    g2 = np.concatenate([np.cos(ang2), -np.sin(ang2)], axis=-1) / math.sqrt(S)
    return n1, rep, jnp.asarray(g1), jnp.asarray(g2.astype(np.float32))


def _seq_dft(pq, tables):
    B, _, S, D = pq.shape
    n1, rep, g1, g2 = tables
    n2 = DFT_N2
    sb = kb = DFT_BLOCK
    pin = jnp.asarray(_regroup_matrix(SUBLANE, sb)).astype(BF16)
    pout = jnp.asarray(_regroup_matrix(sb, SUBLANE)).astype(BF16)
    x = pq.reshape(B, 2, n1, n2, D)
    blk1 = pl.BlockSpec((1, 2, n1, sb, D), lambda b, j: (b, 0, 0, j, 0))
    t = pl.pallas_call(
        functools.partial(_dft1_kernel, rep=rep),
        grid=(B, n2 // sb),
        in_specs=[_const_spec((DFT_ROWS, DFT_ROWS)),
                  _const_spec((SUBLANE * sb, SUBLANE * sb)),
                  _const_spec((SUBLANE * sb, SUBLANE * sb)),
                  blk1],
        out_specs=blk1,
        out_shape=jax.ShapeDtypeStruct((B, 2, n1, n2, D), BF16),
        compiler_params=_params(2),
        name="dft_stage1",
    )(g1.astype(BF16), pin, pout, x)
    y = pl.pallas_call(
        _dft2_kernel,
        grid=(B, n1 // kb),
        in_specs=[pl.BlockSpec((kb, n2, 2 * n2), lambda b, j: (j, 0, 0)),
                  _const_spec((SUBLANE * kb, SUBLANE * kb)),
                  pl.BlockSpec((1, 2, kb, n2, D), lambda b, j: (b, 0, j, 0, 0))],
        out_specs=pl.BlockSpec((1, n2, kb, D), lambda b, j: (b, 0, j, 0)),
        out_shape=jax.ShapeDtypeStruct((B, n2, n1, D), BF16),
        compiler_params=_params(2),
        name="dft_stage2",
    )(g2.astype(BF16), pout, t)
    return y.reshape(B, S, D)


def _mix_kernel(h_ref, xn_ref, am_ref, ap_ref, an_ref, vm_ref, vp_ref, vn_ref, yc_ref,
                band_ref, wg_ref, bg_ref, pscale_ref, cw_ref, cb_ref, lng_ref, lnb_ref,
                cow_ref, wo_ref, o_ref, vext, cbuf, *, seq_len):
    tile, d = h_ref.shape[1], h_ref.shape[2]
    gc = d // N_GROUPS
    i = pl.program_id(1)
    first = i == 0
    last = i == pl.num_programs(1) - 1

    vext[0:HALO] = jnp.where(first, 0.0, vp_ref[0])
    vext[HALO:HALO + tile] = vm_ref[0]
    vext[HALO + tile:] = jnp.where(last, 0.0, vn_ref[0])

    h = h_ref[0]
    xn = xn_ref[0]

    def gate(br):
        cs = slice(br * d, (br + 1) * d)
        return jax.nn.sigmoid(_dot(xn, wg_ref[:, cs]) + bg_ref[:, cs])

    rc = 8
    nj = d // LANE
    for c in range(tile // rc):
        base = c * rc + HALO - CONV_PAD
        acc = vext[base:base + rc] * cw_ref[0]
        for j in range(1, CONV_WIDTH):
            acc = acc + vext[base + j:base + j + rc] * cw_ref[j]
        cbuf[c * rc * nj:(c + 1) * rc * nj, :] = acc.reshape(rc * nj, LANE)
    cv = jnp.concatenate([cbuf[pl.ds(j, tile, stride=nj), :] for j in range(nj)], axis=-1)
    cv = cv + cb_ref[...]
    mu = jnp.mean(cv, axis=-1, keepdims=True)
    xc = cv - mu
    var = jnp.mean(xc * xc, axis=-1, keepdims=True)
    ln = xc * lax.rsqrt(var + EPS) * lng_ref[...] + lnb_ref[...]
    y_b = _dot((ln * jax.nn.sigmoid(ln)).astype(BF16), cow_ref[...])
    merged = gate(1) * y_b

    a_main = am_ref[0]
    a_prev = ap_ref[0]
    a_next = an_ref[0]
    pad = jnp.zeros((band_ref.shape[-1] - tile - 2 * HALO, d), BF16)
    a_ext = jnp.concatenate([a_main,
                             jnp.where(first, jnp.zeros_like(a_prev), a_prev),
                             jnp.where(last, jnp.zeros_like(a_next), a_next),
                             pad], axis=0)
    a_f32 = a_main.astype(F32)
    t = i * tile + lax.broadcasted_iota(jnp.int32, (tile, gc), 0)
    ya = []
    for gi, win in enumerate(POOL_WINDOWS):
        cs = slice(gi * gc, (gi + 1) * gc)
        half = win // 2
        acc = _dot(band_ref[gi], a_ext[:, cs])
        lo = jnp.maximum(t - half, 0)
        hi = jnp.minimum(t + half - 1, seq_len - 1)
        cnt = (hi - lo + 1).astype(F32)
        ya.append(acc / cnt - a_f32[:, cs])
    y_a = jnp.concatenate(ya, axis=-1) * pscale_ref[...]
    merged = merged + gate(0) * y_a
    merged = merged + gate(2) * yc_ref[0].astype(F32)

    o_ref[0] = h + _dot(merged.astype(BF16), wo_ref[...])


def _pool_bands(tile):
    width = -(-(tile + 2 * HALO) // LANE) * LANE
    bands = np.zeros((len(POOL_WINDOWS), tile, width), np.float32)
    for gi, win in enumerate(POOL_WINDOWS):
        for t in range(tile):
            for src in range(t - win // 2, t + win // 2):
                if src < 0:
                    col = tile + HALO + src
                elif src >= tile:
                    col = tile + HALO + (src - tile)
                else:
                    col = src
                bands[gi, t, col] = 1.0
    return bands


def _mix(h, xn, a, v, yc, wg, bg, pscale, cw, cb, lng, lnb, cow, wo, tile):
    B, S, D = h.shape
    bands = jnp.asarray(_pool_bands(tile)).astype(BF16)
    nh = tile // HALO
    n_halo_blocks = S // HALO
    tok = lambda b, i: (b, i, 0)
    prev = lambda b, i: (b, jnp.maximum(i * nh - 1, 0), 0)
    nxt = lambda b, i: (b, jnp.minimum((i + 1) * nh, n_halo_blocks - 1), 0)
    main_spec = pl.BlockSpec((1, tile, D), tok)
    prev_spec = pl.BlockSpec((1, HALO, D), prev)
    next_spec = pl.BlockSpec((1, HALO, D), nxt)
    nj = D // LANE
    v = v.reshape(B, S, nj, LANE)
    vmain_spec = pl.BlockSpec((1, tile, nj, LANE), lambda b, i: tok(b, i) + (0,))
    vprev_spec = pl.BlockSpec((1, HALO, nj, LANE), lambda b, i: prev(b, i) + (0,))
    vnext_spec = pl.BlockSpec((1, HALO, nj, LANE), lambda b, i: nxt(b, i) + (0,))
    cw = cw.reshape(cw.shape[0], nj, LANE)
    return pl.pallas_call(
        functools.partial(_mix_kernel, seq_len=S),
        grid=(B, S // tile),
        in_specs=[main_spec,
                  main_spec,
                  main_spec, prev_spec, next_spec,
                  vmain_spec, vprev_spec, vnext_spec,
                  main_spec,
                  _const_spec(bands.shape),
                  _const_spec((D, N_BRANCH * D)),
                  _const_spec((1, N_BRANCH * D)),
                  _const_spec((1, D)),
                  _const_spec((CONV_WIDTH + 1, nj, LANE)),
                  _const_spec((1, D)),
                  _const_spec((1, D)),
                  _const_spec((1, D)),
                  _const_spec((D, D)),
                  _const_spec((D, D))],
        out_specs=main_spec,
        out_shape=jax.ShapeDtypeStruct((B, S, D), F32),
        scratch_shapes=[pltpu.VMEM((tile + 2 * HALO, nj, LANE), F32),
                        pltpu.VMEM((tile * nj, LANE), F32)],
        compiler_params=_params(2),
        name="mix",
    )(h, xn, a, a, a, v, v, v, yc, bands, wg, bg, pscale, cw, cb, lng, lnb, cow, wo)


def _ffn_kernel(h_ref, p_ref, gff_ref, w1_ref, w2_ref, gple_ref, wpg_ref, wpp_ref, gfin_ref,
                o_ref, *, final):
    h = h_ref[0]
    xn = _rms(h, gff_ref[...]).astype(BF16)
    hf = jnp.maximum(_dot(xn, w1_ref[...]), 0.0)
    h = h + _dot((hf * hf).astype(BF16), w2_ref[...])
    pg = jax.nn.sigmoid(_dot(_rms(h, gple_ref[...]).astype(BF16), wpg_ref[...]))
    h = h + _dot(p_ref[0, 0].astype(BF16), wpp_ref[...]) * pg
    if final:
        h = _rms(h, gfin_ref[...])
    o_ref[0] = h


def _ffn(h, p, layer, gff, w1, w2, gple, wpg, wpp, gfin, tile, final):
    B, S, D = h.shape
    dff = w1.shape[-1]
    dple = p.shape[-1]
    tok = lambda b, i: (b, i, 0)
    return pl.pallas_call(
        functools.partial(_ffn_kernel, final=final),
        grid=(B, S // tile),
        in_specs=[pl.BlockSpec((1, tile, D), tok),
                  pl.BlockSpec((1, 1, tile, dple), lambda b, i: (layer, b, i, 0)),
                  _const_spec((1, D)),
                  _const_spec((D, dff)),
                  _const_spec((dff, D)),
                  _const_spec((1, D)),
                  _const_spec((D, D)),
                  _const_spec((dple, D)),
                  _const_spec((1, D))],
        out_specs=pl.BlockSpec((1, tile, D), tok),
        out_shape=jax.ShapeDtypeStruct((B, S, D), F32),
        compiler_params=_params(2),
        name="ffn_ple",
    )(h, p, gff, w1, w2, gple, wpg, wpp, gfin)


def _trunk(x, p, w, tables):
    depth = w["w_in_a"].shape[0]
    h = x
    for l in range(depth):
        a, v, pq, xn = _inproj(h, w["norm_mix_g"][l], w["w_in_a"][l], w["pool_w"][l], w["wcs"][l], tile=512)
        yc = _seq_dft(pq, tables)
        h = _mix(h, xn, a, v, yc, w["w_in_g"][l], w["b_gate"][l], w["pool_scale"][l],
                 w["conv_w"][l], w["conv_b"][l], w["conv_ln_g"][l], w["conv_ln_b"][l],
                 w["conv_out_w"][l], w["w_o"][l], tile=512)
        h = _ffn(h, p, l, w["norm_ff_g"][l], w["w_ff1"][l], w["w_ff2"][l], w["norm_ple_g"][l],
                 w["w_ple_gate"][l], w["w_ple_proj"][l], w["final_norm_g"], tile=512,
                 final=(l == depth - 1))
    return h


def kernel(x_prompt, x_sample, p_prompt, p_sample, norm_mix_g, w_in, b_gate, pool_w, pool_scale,
           conv_w, conv_b, conv_ln_g, conv_ln_b, conv_out_w, fnet_w, w_o, norm_ff_g, w_ff1, w_ff2,
           norm_ple_g, w_ple_gate, w_ple_proj, final_norm_g):
    L, D, _ = w_in.shape
    row = lambda t: t.reshape(t.shape[0], 1, t.shape[-1])
    w = {
        "norm_mix_g": row(norm_mix_g),
        "w_in_a": w_in[:, :, :4 * D].astype(BF16),
        "w_in_g": w_in[:, :, 4 * D:].astype(BF16),
        "b_gate": row(b_gate),
        "pool_w": pool_w.astype(BF16),
        "pool_scale": row(pool_scale),
        "conv_w": jnp.pad(conv_w, ((0, 0), (0, 1), (0, 0))),
        "conv_b": row(conv_b),
        "conv_ln_g": row(conv_ln_g),
        "conv_ln_b": row(conv_ln_b),
        "conv_out_w": conv_out_w.astype(BF16),
        "wcs": _fold_fnet(fnet_w).astype(BF16),
        "w_o": w_o.astype(BF16),
        "norm_ff_g": row(norm_ff_g),
        "w_ff1": w_ff1.astype(BF16),
        "w_ff2": w_ff2.astype(BF16),
        "norm_ple_g": row(norm_ple_g),
        "w_ple_gate": w_ple_gate.astype(BF16),
        "w_ple_proj": w_ple_proj.astype(BF16),
        "final_norm_g": final_norm_g.reshape(1, D),
    }
    y_prompt = _trunk(x_prompt, p_prompt, w, _dft_tables(x_prompt.shape[1]))
    y_sample = _trunk(x_sample, p_sample, w, _dft_tables(x_sample.shape[1]))
    return (y_prompt, y_sample)
```
